```python
import numpy as np
import jax
import jax.numpy as jnp
from jax import lax

D_MODEL = 1024
BATCH = 4
SEQ = 4096
DEPTH = 1
DEC_BATCH = 32
DEC_SEQ = 1
PAST_LEN = 16384
PAGE_SIZE = 128

HEAD_DIM = 64
H_A = 8
H_B = 8
H_IDX = 8
D_IDX = 64
ROT_DIM = HEAD_DIM // 4
ROPE_THETA = 500000.0
TOPK_MAX = 256
Q_BLOCK = 128
N_EXPERTS = 32
TOP_K_EXPERTS = 4
D_FF = D_MODEL
SWIGLU_LIMIT = 7.0
SWIGLU_ALPHA = 1.702
EPS = 1e-6
N_ADA = 6

IN_SPLITS = (H_A * HEAD_DIM, H_A * HEAD_DIM, H_A * HEAD_DIM,
             H_IDX * D_IDX, D_IDX, H_IDX,
             H_B * HEAD_DIM, H_B * HEAD_DIM, H_B * HEAD_DIM, H_B,
             D_MODEL, D_MODEL)
D_IN = sum(IN_SPLITS)

kernel_name = 'hybrid_dsa_fox_moe_decode_step'


def rmsnorm(x, g):
    xf = x.astype(jnp.float32)
    y = xf * lax.rsqrt(jnp.mean(xf * xf, axis=-1, keepdims=True) + EPS)
    return (y * g.astype(jnp.float32)).astype(x.dtype)


def rope_partial(x, pos):
    half = ROT_DIM // 2
    inv_freq = ROPE_THETA ** (-jnp.arange(half, dtype=jnp.float32) * 2.0 / ROT_DIM)
    ang = pos.astype(jnp.float32)[:, None] * inv_freq[None, :]
    cos = jnp.cos(ang)[None, :, None, :]
    sin = jnp.sin(ang)[None, :, None, :]
    xr = x[..., :ROT_DIM].astype(jnp.float32)
    x1, x2 = xr[..., :half], xr[..., half:]
    rot = jnp.concatenate([x1 * cos - x2 * sin, x2 * cos + x1 * sin], axis=-1).astype(x.dtype)
    return jnp.concatenate([rot, x[..., ROT_DIM:]], axis=-1)


def adaln(c, w_ada, b_ada):
    mod = jax.nn.silu(c) @ w_ada + b_ada
    return jnp.split(mod, N_ADA, axis=-1)


def modulate(h, shift, scale):
    return h * (1.0 + scale[:, None, :]) + shift[:, None, :]


def project(h, pos, w_in, b_f, g_qa, g_ka, g_kidx, g_qb, g_kb):
    bsz, t = h.shape[0], h.shape[1]
    offsets = np.cumsum(IN_SPLITS)[:-1].tolist()
    qa, ka, va, qi, ki, wi, qb, kb, vb, fb, ga, gb = jnp.split(h @ w_in, offsets, axis=-1)
    heads = lambda z, n, d: z.reshape(bsz, t, n, d)
    q_a = rope_partial(rmsnorm(heads(qa, H_A, HEAD_DIM), g_qa), pos)
    k_a = rope_partial(rmsnorm(heads(ka, H_A, HEAD_DIM), g_ka), pos)
    v_a = heads(va, H_A, HEAD_DIM)
    q_i = rope_partial(heads(qi, H_IDX, D_IDX), pos)
    k_i = rope_partial(rmsnorm(ki, g_kidx)[:, :, None, :], pos)[:, :, 0, :]
    w_i = wi * (H_IDX ** -0.5 * D_IDX ** -0.5)
    q_b = rmsnorm(heads(qb, H_B, HEAD_DIM), g_qb)
    k_b = rmsnorm(heads(kb, H_B, HEAD_DIM), g_kb)
    v_b = heads(vb, H_B, HEAD_DIM)
    logf = jax.nn.log_sigmoid((fb + b_f).astype(jnp.float32))
    return (q_a, k_a, v_a, q_i, k_i, w_i, q_b, k_b, v_b, logf,
            jax.nn.sigmoid(ga), jax.nn.sigmoid(gb))


def to_blocks(x):
    b, s = x.shape[0], x.shape[1]
    return jnp.moveaxis(x.reshape(b, s // Q_BLOCK, Q_BLOCK, *x.shape[2:]), 1, 0)


def from_blocks(y):
    nb, b, qb = y.shape[0], y.shape[1], y.shape[2]
    return jnp.moveaxis(y, 0, 1).reshape(b, nb * qb, *y.shape[3:])


def dsa_attend(q, q_i, w_i, pos_q, k_i_all, gather_kv):
    n_keys = k_i_all.shape[1]
    n_sel = min(TOPK_MAX, n_keys // 4)
    dots = jnp.einsum('btid,bsd->btis', q_i, k_i_all)
    score = jnp.einsum('bti,btis->bts', w_i, jax.nn.relu(dots)).astype(jnp.float32)
    admissible = jnp.arange(n_keys)[None, :] <= pos_q[:, None]
    score = jnp.where(admissible[None], score, -jnp.inf)
    top_score, top_idx = lax.top_k(score, n_sel)
    k_sel, v_sel = gather_kv(top_idx)
    logits = jnp.einsum('bthd,btkhd->bthk', q, k_sel).astype(jnp.float32) * HEAD_DIM ** -0.5
    logits = jnp.where(jnp.isfinite(top_score)[:, :, None, :], logits, -jnp.inf)
    p = jax.nn.softmax(logits, axis=-1).astype(v_sel.dtype)
    return jnp.einsum('bthk,btkhd->bthd', p, v_sel)


def fox_attend(q, d_q, pos_q, k, v, d_k):
    n_keys = k.shape[1]
    logits = jnp.einsum('bthd,bshd->bhts', q, k).astype(jnp.float32) * HEAD_DIM ** -0.5
    bias = jnp.swapaxes(d_q, 1, 2)[:, :, :, None] - jnp.swapaxes(d_k, 1, 2)[:, :, None, :]
    causal = jnp.arange(n_keys)[None, :] <= pos_q[:, None]
    logits = jnp.where(causal[None, None], logits + bias, -jnp.inf)
    p = jax.nn.softmax(logits, axis=-1).astype(v.dtype)
    return jnp.einsum('bhts,bshd->bthd', p, v)


def prompt_dsa(q_a, k_a, v_a, q_i, k_i, w_i, pos):
    take = jax.vmap(lambda rows, idx: rows[idx])
    gather_kv = lambda idx: (take(k_a, idx), take(v_a, idx))
    blk = lambda a: dsa_attend(a[0], a[1], a[2], a[3], k_i, gather_kv)
    out = lax.map(blk, (to_blocks(q_a), to_blocks(q_i), to_blocks(w_i), pos.reshape(-1, Q_BLOCK)))
    return from_blocks(out)


def prompt_fox(q_b, k_b, v_b, logf, pos):
    d = jnp.cumsum(logf.astype(jnp.float32), axis=1)
    blk = lambda a: fox_attend(a[0], a[1], a[2], k_b, v_b, d)
    out = lax.map(blk, (to_blocks(q_b), to_blocks(d), pos.reshape(-1, Q_BLOCK)))
    return from_blocks(out)


def sample_dsa(q_a, k_a, v_a, q_i, k_i, w_i, pos, cache_k, cache_v, cache_kidx, page_table):
    bsz = q_a.shape[0]
    k_i_past = cache_kidx[page_table].reshape(bsz, PAST_LEN, D_IDX)
    k_i_all = jnp.concatenate([k_i_past, k_i.astype(k_i_past.dtype)], axis=1)
    take = jax.vmap(lambda rows, idx: rows[idx])

    def gather_kv(idx):
        in_cache = (idx < PAST_LEN)[..., None, None]
        idx_c = jnp.minimum(idx, PAST_LEN - 1)
        phys = take(page_table, idx_c // PAGE_SIZE)
        slot = idx_c % PAGE_SIZE
        idx_n = jnp.clip(idx - PAST_LEN, 0, DEC_SEQ - 1)
        k_sel = jnp.where(in_cache, cache_k[phys, slot], take(k_a, idx_n))
        v_sel = jnp.where(in_cache, cache_v[phys, slot], take(v_a, idx_n))
        return k_sel, v_sel

    return dsa_attend(q_a, q_i, w_i, pos, k_i_all, gather_kv)


def sample_fox(q_b, k_b, v_b, logf, pos, cache_k, cache_v, cache_logf, page_table):
    bsz = q_b.shape[0]
    k_past = cache_k[page_table].reshape(bsz, PAST_LEN, H_B, HEAD_DIM)
    v_past = cache_v[page_table].reshape(bsz, PAST_LEN, H_B, HEAD_DIM)
    lf_past = cache_logf[page_table].reshape(bsz, PAST_LEN, H_B)
    k_all = jnp.concatenate([k_past, k_b.astype(k_past.dtype)], axis=1)
    v_all = jnp.concatenate([v_past, v_b.astype(v_past.dtype)], axis=1)
    d_all = jnp.cumsum(jnp.concatenate([lf_past.astype(jnp.float32), logf.astype(jnp.float32)], axis=1), axis=1)
    return fox_attend(q_b, d_all[:, PAST_LEN:], pos, k_all, v_all, d_all)


def mixer_out(out_a, out_b, gate_a, gate_b, w_oa, w_ob, w_out):
    bsz, t = out_a.shape[0], out_a.shape[1]
    ya = out_a.reshape(bsz, t, H_A * HEAD_DIM) @ w_oa
    yb = out_b.reshape(bsz, t, H_B * HEAD_DIM) @ w_ob
    return (gate_a * ya + gate_b * yb) @ w_out


def moe(x, w_router, b_router, w_up, b_up, w_down, b_down):
    bsz, t, d = x.shape
    xf = x.reshape(bsz * t, d)
    logits = (xf @ w_router + b_router).astype(jnp.float32)
    top_v, top_i = lax.top_k(logits, TOP_K_EXPERTS)
    probs = jax.nn.softmax(top_v, axis=-1)
    combine = jnp.sum(jax.nn.one_hot(top_i, N_EXPERTS, dtype=jnp.float32) * probs[..., None], axis=1)
    out = jnp.zeros((bsz * t, d), jnp.float32)
    for e in range(N_EXPERTS):
        hu = xf @ w_up[e] + b_up[e]
        g = jnp.minimum(hu[:, :D_FF], SWIGLU_LIMIT)
        lin = jnp.clip(hu[:, D_FF:], -SWIGLU_LIMIT, SWIGLU_LIMIT)
        act = (lin + 1.0) * g * jax.nn.sigmoid(SWIGLU_ALPHA * g)
        out = out + combine[:, e:e + 1] * (act @ w_down[e] + b_down[e]).astype(jnp.float32)
    return out.astype(x.dtype).reshape(bsz, t, d)


def setup_inputs(seed: int = 0) -> dict:
    key = jax.random.key(seed)
    ks = jax.random.split(key, 32)
    f32 = jnp.float32
    n_pages = PAST_LEN // PAGE_SIZE
    n_used = DEC_BATCH * n_pages
    n_pool = n_used + max(1, n_used // 4)
    nrm = lambda k, shape, s: jax.random.normal(k, shape, f32) * s
    hd_a = H_A * HEAD_DIM
    hd_b = H_B * HEAD_DIM
    return {
        'x_prompt': nrm(ks[0], (BATCH, SEQ, D_MODEL), 1.0),
        'x_sample': nrm(ks[1], (DEC_BATCH, DEC_SEQ, D_MODEL), 1.0),
        'c_prompt': nrm(ks[2], (BATCH, D_MODEL), 1.0),
        'c_sample': nrm(ks[3], (DEC_BATCH, D_MODEL), 1.0),
        'cache_k_a': nrm(ks[4], (DEPTH, n_pool, PAGE_SIZE, H_A, HEAD_DIM), 1.0),
        'cache_v_a': nrm(ks[5], (DEPTH, n_pool, PAGE_SIZE, H_A, HEAD_DIM), 1.0),
        'cache_kidx_a': nrm(ks[6], (DEPTH, n_pool, PAGE_SIZE, D_IDX), 1.0),
        'cache_k_b': nrm(ks[7], (DEPTH, n_pool, PAGE_SIZE, H_B, HEAD_DIM), 1.0),
        'cache_v_b': nrm(ks[8], (DEPTH, n_pool, PAGE_SIZE, H_B, HEAD_DIM), 1.0),
        'cache_logf_b': jax.nn.log_sigmoid(2.0 + nrm(ks[9], (DEPTH, n_pool, PAGE_SIZE, H_B), 0.5)),
        'page_table': jax.random.permutation(ks[10], n_pool)[:n_used].reshape(DEC_BATCH, n_pages).astype(jnp.int32),
        'w_ada': nrm(ks[11], (DEPTH, D_MODEL, N_ADA * D_MODEL), D_MODEL ** -0.5),
        'b_ada': nrm(ks[12], (DEPTH, N_ADA * D_MODEL), 0.02),
        'g_norm1': 1.0 + nrm(ks[13], (DEPTH, D_MODEL), 0.02),
        'w_in': nrm(ks[14], (DEPTH, D_MODEL, D_IN), D_MODEL ** -0.5),
        'b_f': 2.0 + nrm(ks[15], (DEPTH, H_B), 0.5),
        'g_qa': 1.0 + nrm(ks[16], (DEPTH, HEAD_DIM), 0.02),
        'g_ka': 1.0 + nrm(ks[17], (DEPTH, HEAD_DIM), 0.02),
        'g_kidx': 1.0 + nrm(ks[18], (DEPTH, D_IDX), 0.02),
        'g_qb': 1.0 + nrm(ks[19], (DEPTH, HEAD_DIM), 0.02),
        'g_kb': 1.0 + nrm(ks[20], (DEPTH, HEAD_DIM), 0.02),
        'w_oa': nrm(ks[21], (DEPTH, hd_a, D_MODEL), hd_a ** -0.5),
        'w_ob': nrm(ks[22], (DEPTH, hd_b, D_MODEL), hd_b ** -0.5),
        'w_out': nrm(ks[23], (DEPTH, D_MODEL, D_MODEL), D_MODEL ** -0.5),
        'g_norm2': 1.0 + nrm(ks[24], (DEPTH, D_MODEL), 0.02),
        'w_router': nrm(ks[25], (DEPTH, D_MODEL, N_EXPERTS), D_MODEL ** -0.5),
        'b_router': nrm(ks[26], (DEPTH, N_EXPERTS), 0.01),
        'w_up': nrm(ks[27], (DEPTH, N_EXPERTS, D_MODEL, 2 * D_FF), D_MODEL ** -0.5),
        'b_up': nrm(ks[28], (DEPTH, N_EXPERTS, 2 * D_FF), 0.01),
        'w_down': nrm(ks[29], (DEPTH, N_EXPERTS, D_FF, D_MODEL), D_FF ** -0.5),
        'b_down': nrm(ks[30], (DEPTH, N_EXPERTS, D_MODEL), 0.01),
    }


def reference(x_prompt, x_sample, c_prompt, c_sample, cache_k_a, cache_v_a, cache_kidx_a,
              cache_k_b, cache_v_b, cache_logf_b, page_table, w_ada, b_ada, g_norm1, w_in, b_f,
              g_qa, g_ka, g_kidx, g_qb, g_kb, w_oa, w_ob, w_out, g_norm2, w_router, b_router,
              w_up, b_up, w_down, b_down):
    pos_p = jnp.arange(SEQ, dtype=jnp.int32)
    pos_s = PAST_LEN + jnp.arange(DEC_SEQ, dtype=jnp.int32)
    xp, xs = x_prompt, x_sample
    nk_a_p, nv_a_p, nki_p, nk_b_p, nv_b_p, nlf_p = [], [], [], [], [], []
    nk_a_s, nv_a_s, nki_s, nk_b_s, nv_b_s, nlf_s = [], [], [], [], [], []
    for l in range(DEPTH):
        proj_w = (w_in[l], b_f[l], g_qa[l], g_ka[l], g_kidx[l], g_qb[l], g_kb[l])
        out_w = (w_oa[l], w_ob[l], w_out[l])
        moe_w = (w_router[l], b_router[l], w_up[l], b_up[l], w_down[l], b_down[l])

        sh1, sc1, gt1, sh2, sc2, gt2 = adaln(c_prompt, w_ada[l], b_ada[l])
        h = modulate(rmsnorm(xp, g_norm1[l]), sh1, sc1)
        q_a, k_a, v_a, q_i, k_i, w_i, q_b, k_b, v_b, logf, ga, gb = project(h, pos_p, *proj_w)
        out_a = prompt_dsa(q_a, k_a, v_a, q_i, k_i, w_i, pos_p)
        out_b = prompt_fox(q_b, k_b, v_b, logf, pos_p)
        xp = xp + gt1[:, None, :] * mixer_out(out_a, out_b, ga, gb, *out_w)
        h2 = modulate(rmsnorm(xp, g_norm2[l]), sh2, sc2)
        xp = xp + gt2[:, None, :] * moe(h2, *moe_w)
        nk_a_p.append(k_a); nv_a_p.append(v_a); nki_p.append(k_i)
        nk_b_p.append(k_b); nv_b_p.append(v_b); nlf_p.append(logf)

        sh1, sc1, gt1, sh2, sc2, gt2 = adaln(c_sample, w_ada[l], b_ada[l])
        h = modulate(rmsnorm(xs, g_norm1[l]), sh1, sc1)
        q_a, k_a, v_a, q_i, k_i, w_i, q_b, k_b, v_b, logf, ga, gb = project(h, pos_s, *proj_w)
        out_a = sample_dsa(q_a, k_a, v_a, q_i, k_i, w_i, pos_s,
                           cache_k_a[l], cache_v_a[l], cache_kidx_a[l], page_table)
        out_b = sample_fox(q_b, k_b, v_b, logf, pos_s,
                           cache_k_b[l], cache_v_b[l], cache_logf_b[l], page_table)
        xs = xs + gt1[:, None, :] * mixer_out(out_a, out_b, ga, gb, *out_w)
        h2 = modulate(rmsnorm(xs, g_norm2[l]), sh2, sc2)
        xs = xs + gt2[:, None, :] * moe(h2, *moe_w)
        nk_a_s.append(k_a); nv_a_s.append(v_a); nki_s.append(k_i)
        nk_b_s.append(k_b); nv_b_s.append(v_b); nlf_s.append(logf)

    return (xp, xs,
            jnp.stack(nk_a_p), jnp.stack(nv_a_p), jnp.stack(nki_p),
            jnp.stack(nk_b_p), jnp.stack(nv_b_p), jnp.stack(nlf_p),
            jnp.stack(nk_a_s), jnp.stack(nv_a_s), jnp.stack(nki_s),
            jnp.stack(nk_b_s), jnp.stack(nv_b_s), jnp.stack(nlf_s))
```

```python
import functools

import numpy as np
import jax
import jax.numpy as jnp
from jax import lax
from jax.experimental import pallas as pl
from jax.experimental.pallas import tpu as pltpu

F32 = jnp.float32
BF16 = jnp.bfloat16
I32 = jnp.int32

D_MODEL = 1024
HEAD_DIM = 64
N_HEADS = 8
HD = N_HEADS * HEAD_DIM
D_IDX = 64
ROT_DIM = HEAD_DIM // 4
ROPE_THETA = 500000.0
TOPK_MAX = 256
N_EXPERTS = 32
TOP_K_EXPERTS = 4
D_FF = D_MODEL
SWIGLU_LIMIT = 7.0
SWIGLU_ALPHA = 1.702
EPS = 1e-6
N_ADA = 6
PAGE_SIZE = 128
LANES = 128
W_IDX_SCALE = N_HEADS ** -0.5 * D_IDX ** -0.5
QK_SCALE = HEAD_DIM ** -0.5

C_QA, C_KA, C_VA, C_QI, C_QB, C_KB, C_VB = (i * HD for i in range(7))
C_GA = 7 * HD
C_GB = C_GA + D_MODEL
C_SM = C_GB + D_MODEL
D_IN_PAD = C_SM + LANES
L_WI = D_IDX
L_LF = D_IDX + N_HEADS
L_D = L_LF + N_HEADS

NEG_INF = float("-inf")
INT_MIN = -2 ** 31
KEY_NEG_INF = int(np.int32(np.uint32(0xFF800000) ^ np.uint32(0x7FFFFFFF)))

VMEM_LIMIT = 56 * 1024 * 1024


def _cparams(sem):
    return pltpu.CompilerParams(dimension_semantics=sem, vmem_limit_bytes=VMEM_LIMIT)


def _sigmoid(x):
    return 1.0 / (1.0 + jnp.exp(-x))


def _dot(a, b):
    return jnp.dot(a, b, preferred_element_type=F32)


def _dot_nt(a, b):
    return lax.dot_general(a, b, (((1,), (1,)), ((), ())), preferred_element_type=F32)


def _sort_key(x):
    b = pltpu.bitcast(x, I32)
    return b ^ ((b >> 31) & jnp.int32(0x7FFFFFFF))


def _adaln_kernel(c_ref, w_ref, b_ref, o_ref):
    c = c_ref[...]
    s = c * _sigmoid(c)
    o_ref[...] = _dot(s.astype(BF16), w_ref[...].astype(BF16)) + b_ref[...]


def _adaln(c, w_ada, b_ada):
    rows, d = c.shape
    n = w_ada.shape[1]
    tn = 1536
    return pl.pallas_call(
        _adaln_kernel,
        grid=(n // tn,),
        in_specs=[pl.BlockSpec((rows, d), lambda j: (0, 0)),
                  pl.BlockSpec((d, tn), lambda j: (0, j)),
                  pl.BlockSpec((1, tn), lambda j: (0, j))],
        out_specs=pl.BlockSpec((rows, tn), lambda j: (0, j)),
        out_shape=jax.ShapeDtypeStruct((rows, n), F32),
        compiler_params=_cparams(("arbitrary",)),
        name="adaln",
    )(c, w_ada, b_ada.reshape(1, n))


def _rope(z, cos, sp, sm):
    w = z.shape[1]
    return z * cos + pltpu.roll(z, 8, 1) * sp + pltpu.roll(z, w - 8, 1) * sm


def _tile4(t):
    return jnp.concatenate([t, t, t, t], axis=1)


def _proj_kernel(x_ref, gn_ref, sh_ref, sc_ref, w_ref, gains_ref, gsm_ref, bf_ref,
                 cos_ref, sp_ref, sm_ref, gmat_ref, gsmat_ref,
                 ka_o, va_o, kb_o, vb_o, small_o,
                 qa_o, qi_o, qb_o, kab_o, vab_o, kbb_o, vbb_o, kid_o, ga_o, gb_o,
                 carry_ref):
    i = pl.program_id(1)
    tm = x_ref.shape[1]

    @pl.when(i == 0)
    def _():
        carry_ref[...] = jnp.zeros_like(carry_ref)

    x = x_ref[0]
    ms = jnp.mean(x * x, axis=-1, keepdims=True)
    y = x * lax.rsqrt(ms + EPS) * gn_ref[...]
    h = y * (1.0 + sc_ref[0]) + sh_ref[0]
    hb = h.astype(BF16)

    cos128, sp128, sm128 = cos_ref[...], sp_ref[...], sm_ref[...]
    cos, sp, sm = _tile4(cos128), _tile4(sp128), _tile4(sm128)
    gmat = gmat_ref[...]

    def sect(c0, width=HD):
        return _dot(hb, w_ref[:, c0:c0 + width])

    def headnorm(z, g):
        msq = _dot((z * z).astype(BF16), gmat)
        return z * lax.rsqrt(msq + EPS) * g

    qa = _rope(headnorm(sect(C_QA), gains_ref[0:1, :]), cos, sp, sm)
    qa_o[0] = (qa * QK_SCALE).astype(BF16)
    ka = _rope(headnorm(sect(C_KA), gains_ref[1:2, :]), cos, sp, sm)
    ka_o[0] = ka
    kab_o[0] = ka.astype(BF16)
    va = sect(C_VA)
    va_o[0] = va
    vab_o[0] = va.astype(BF16)
    qi_o[0] = _rope(sect(C_QI), cos, sp, sm).astype(BF16)
    qb_o[0] = (headnorm(sect(C_QB), gains_ref[2:3, :]) * QK_SCALE).astype(BF16)
    kb = headnorm(sect(C_KB), gains_ref[3:4, :])
    kb_o[0] = kb
    kbb_o[0] = kb.astype(BF16)
    vb = sect(C_VB)
    vb_o[0] = vb
    vbb_o[0] = vb.astype(BF16)
    ga_o[0] = _sigmoid(sect(C_GA, D_MODEL)).astype(BF16)
    gb_o[0] = _sigmoid(sect(C_GB, D_MODEL)).astype(BF16)

    zs = sect(C_SM, LANES)
    lane = lax.broadcasted_iota(I32, (tm, LANES), 1)
    msq = _dot((zs * zs).astype(BF16), gsmat_ref[...])
    ki = _rope(zs * lax.rsqrt(msq + EPS) * gsm_ref[...], cos128, sp128, sm128)
    kid_o[0] = (ki + pltpu.roll(ki, D_IDX, 1)).astype(BF16)
    t = zs + bf_ref[...]
    logf = jnp.minimum(t, 0.0) - jnp.log1p(jnp.exp(-jnp.abs(t)))
    lf_only = jnp.where((lane >= L_LF) & (lane < L_D), logf, 0.0)
    r_i = lax.broadcasted_iota(I32, (tm, tm), 0)
    c_i = lax.broadcasted_iota(I32, (tm, tm), 1)
    tri = (c_i <= r_i).astype(F32)
    dcum = jnp.dot(tri, lf_only, preferred_element_type=F32,
                   precision=lax.Precision.HIGHEST) + carry_ref[...]
    carry_ref[...] = dcum[tm - 1:tm, :]
    small = (ki + jnp.where((lane >= L_WI) & (lane < L_LF), zs * W_IDX_SCALE, 0.0)
             + lf_only + pltpu.roll(dcum, N_HEADS, 1))
    small_o[0] = small


def _rope_tables(pos):
    half = ROT_DIM // 2
    inv_freq = ROPE_THETA ** (-jnp.arange(half, dtype=F32) * 2.0 / ROT_DIM)
    ang = pos.astype(F32)[:, None] * inv_freq[None, :]
    cos8, sin8 = jnp.cos(ang), jnp.sin(ang)
    t = pos.shape[0]
    one = jnp.ones((t, HEAD_DIM - ROT_DIM), F32)
    zero = jnp.zeros((t, HEAD_DIM - ROT_DIM), F32)
    z8 = jnp.zeros((t, half), F32)
    cos64 = jnp.concatenate([cos8, cos8, one], axis=1)
    sp64 = jnp.concatenate([z8, sin8, zero], axis=1)
    sm64 = jnp.concatenate([-sin8, z8, zero], axis=1)
    dup = lambda a: jnp.concatenate([a, a], axis=1)
    return dup(cos64), dup(sp64), dup(sm64)


def _proj_weights(w_in, b_f, g_qa, g_ka, g_kidx, g_qb, g_kb):
    offs = np.cumsum((HD, HD, HD, HD, D_IDX, N_HEADS, HD, HD, HD, N_HEADS, D_MODEL, D_MODEL))[:-1].tolist()
    qa, ka, va, qi, ki, wi, qb, kb, vb, fb, ga, gb = jnp.split(w_in, offs, axis=1)
    pad = jnp.zeros((w_in.shape[0], LANES - D_IDX - 2 * N_HEADS), w_in.dtype)
    w = jnp.concatenate([qa, ka, va, qi, qb, kb, vb, ga, gb, ki, wi, fb, pad], axis=1).astype(BF16)
    tile8 = lambda g: jnp.tile(g, N_HEADS)
    gains = jnp.stack([tile8(g_qa), tile8(g_ka), tile8(g_qb), tile8(g_kb)]
                      + [jnp.zeros((HD,), F32)] * 4)
    gsm = jnp.concatenate([g_kidx, jnp.zeros((LANES - D_IDX,), F32)]).reshape(1, LANES)
    bfv = jnp.zeros((LANES,), F32).at[L_LF:L_D].set(b_f).reshape(1, LANES)
    blk = np.kron(np.eye(N_HEADS), np.full((HEAD_DIM, HEAD_DIM), 1.0 / HEAD_DIM))
    gmat = jnp.asarray(blk, BF16)
    gs = np.zeros((LANES, LANES))
    gs[:D_IDX, :D_IDX] = 1.0 / D_IDX
    return w, gains, gsm, bfv, gmat, jnp.asarray(gs, BF16)


def _proj(x, g_norm, shift, scale, pw, tables, tm):
    w, gains, gsm, bfv, gmat, gsmat = pw
    b, t, d = x.shape
    per_row = shift.shape[1] != 1
    mod_spec = (pl.BlockSpec((1, tm, d), lambda bi, i: (bi, i, 0)) if per_row
                else pl.BlockSpec((1, 1, d), lambda bi, i: (bi, 0, 0)))
    const = lambda shape: pl.BlockSpec(shape, lambda bi, i: tuple(0 for _ in shape))
    tab_spec = pl.BlockSpec((tm, LANES), lambda bi, i: (i, 0))
    row = lambda width: pl.BlockSpec((1, tm, width), lambda bi, i: (bi, i, 0))
    f = lambda width, dt: jax.ShapeDtypeStruct((b, t, width), dt)
    return pl.pallas_call(
        _proj_kernel,
        grid=(b, t // tm),
        in_specs=[row(d), const((1, d)), mod_spec, mod_spec, const(w.shape), const(gains.shape),
                  const((1, LANES)), const((1, LANES)), tab_spec, tab_spec, tab_spec,
                  const(gmat.shape), const(gsmat.shape)],
        out_specs=[row(HD)] * 4 + [row(LANES)] + [row(HD)] * 7 + [row(LANES)] + [row(d)] * 2,
        out_shape=[f(HD, F32)] * 4 + [f(LANES, F32)] + [f(HD, BF16)] * 7 + [f(LANES, BF16)]
                  + [f(d, BF16)] * 2,
        scratch_shapes=[pltpu.VMEM((1, LANES), F32)],
        compiler_params=_cparams(("arbitrary", "arbitrary")),
        name="proj",
    )(x, g_norm.reshape(1, d), shift, scale, w, gains, gsm, bfv, *tables, gmat, gsmat)


DSA_TQ = 128
DSA_KC = 512


def _pair_masked(q, h):
    pair = q[:, (h // 2) * LANES:(h // 2 + 1) * LANES].astype(F32)
    lane = lax.broadcasted_iota(I32, pair.shape, 1)
    keep = (lane < HEAD_DIM) if h % 2 == 0 else (lane >= HEAD_DIM)
    return jnp.where(keep, pair, 0.0).astype(BF16)


def _topk_threshold(count_ge, rows, k):
    zero = jnp.zeros((rows, 1), I32)
    t0 = jnp.where(count_ge(zero) >= k, zero, jnp.full((rows, 1), INT_MIN, I32))

    def body(j, t):
        cand = t | (jnp.int32(1) << (30 - j))
        return jnp.where(count_ge(cand) >= k, cand, t)

    return lax.fori_loop(0, 31, body, t0)


def _fold_lanes(m):
    out = m[:, 0:LANES]
    for j in range(1, m.shape[1] // LANES):
        out = out + m[:, j * LANES:(j + 1) * LANES]
    return out


def _dsa_prompt_kernel(qi_ref, small_ref, qa_ref, kid_ref, ka_ref, va_ref, o_ref,
                       key_scr, m_scr, l_scr, acc_scr, *, n_sel):
    qt = pl.program_id(1)
    tq, kc = DSA_TQ, DSA_KC
    nkc = ((qt + 1) * tq + kc - 1) // kc
    small = small_ref[0]
    qi = qi_ref[0]
    qa = qa_ref[0]
    qpos = qt * tq + lax.broadcasted_iota(I32, (tq, kc), 0)
    lane_k = lax.broadcasted_iota(I32, (tq, kc), 1)

    def score_body(c, carry):
        off = pl.multiple_of(c * kc, kc)
        kid = kid_ref[0, pl.ds(off, kc), :]
        acc = jnp.zeros((tq, kc), F32)
        for h in range(N_HEADS):
            d = _dot_nt(_pair_masked(qi, h), kid)
            acc = acc + small[:, L_WI + h:L_WI + h + 1] * jnp.maximum(d, 0.0)
        acc = jnp.where(off + lane_k <= qpos, acc + 0.0, NEG_INF)
        key_scr[:, pl.ds(off, kc)] = _sort_key(acc)
        return carry

    lax.fori_loop(0, nkc, score_body, 0)

    def count(pred):
        def body(c, cnt):
            off = pl.multiple_of(c * kc, kc)
            return cnt + _fold_lanes(pred(key_scr[:, pl.ds(off, kc)]).astype(I32))
        cnt = lax.fori_loop(0, nkc, body, jnp.zeros((tq, LANES), I32))
        return jnp.sum(cnt, axis=1, keepdims=True)

    thr = _topk_threshold(lambda cand: count(lambda k: k >= cand), tq, n_sel)
    cnt_gt = count(lambda k: k > thr)
    cnt_ge = count(lambda k: k >= thr)
    need = n_sel - cnt_gt
    tie = (cnt_ge > n_sel) & (thr > KEY_NEG_INF)

    @pl.when(jnp.max(tie.astype(I32)) > 0)
    def _():
        r_i = lax.broadcasted_iota(I32, (LANES, LANES), 0)
        c_i = lax.broadcasted_iota(I32, (LANES, LANES), 1)
        upper = (r_i < c_i).astype(BF16)

        def body(c, seen):
            off = pl.multiple_of(c * LANES, LANES)
            k = key_scr[:, pl.ds(off, LANES)]
            eq = k == thr
            eqf = jnp.where(eq, 1.0, 0.0)
            rank = seen + _dot(eqf.astype(BF16), upper)
            demote = eq & tie & (rank >= need.astype(F32))
            key_scr[:, pl.ds(off, LANES)] = jnp.where(demote, KEY_NEG_INF, k)
            return seen + jnp.sum(eqf, axis=1, keepdims=True)

        lax.fori_loop(0, nkc * (kc // LANES), body, jnp.zeros((tq, 1), F32))

    m_scr[...] = jnp.full(m_scr.shape, NEG_INF, F32)
    l_scr[...] = jnp.zeros(l_scr.shape, F32)
    acc_scr[...] = jnp.zeros(acc_scr.shape, F32)
    qam = [_pair_masked(qa, h) for h in range(N_HEADS)]

    def att_body(c, carry):
        off = pl.multiple_of(c * kc, kc)
        k = key_scr[:, pl.ds(off, kc)]
        sel = (k >= thr) & (k > KEY_NEG_INF)
        for h in range(N_HEADS):
            p0 = (h // 2) * LANES
            kp = ka_ref[0, pl.ds(off, kc), p0:p0 + LANES]
            vp = va_ref[0, pl.ds(off, kc), p0:p0 + LANES]
            s = jnp.where(sel, _dot_nt(qam[h], kp), NEG_INF)
            m_old = m_scr[h]
            m_new = jnp.maximum(m_old, jnp.max(s, axis=1, keepdims=True))
            m_safe = jnp.where(m_new == NEG_INF, 0.0, m_new)
            p = jnp.exp(s - m_safe[:, 0:1])
            alpha = jnp.exp(m_old - m_safe)
            l_scr[h] = alpha * l_scr[h] + jnp.sum(p, axis=1, keepdims=True)
            acc_scr[h] = alpha * acc_scr[h] + _dot(p.astype(BF16), vp)
            m_scr[h] = m_new
        return carry

    lax.fori_loop(0, nkc, att_body, 0)

    lane = lax.broadcasted_iota(I32, (tq, LANES), 1)
    for p in range(N_HEADS // 2):
        lo = acc_scr[2 * p] / l_scr[2 * p]
        hi = acc_scr[2 * p + 1] / l_scr[2 * p + 1]
        o_ref[0, :, p * LANES:(p + 1) * LANES] = jnp.where(lane < HEAD_DIM, lo, hi).astype(o_ref.dtype)


def _dsa_prompt(qi, small, qa, kid, kab, vab):
    b, t, _ = qa.shape
    tq = DSA_TQ
    n_sel = min(TOPK_MAX, t // 4)
    rowq = lambda w: pl.BlockSpec((1, tq, w), lambda bi, i: (bi, i, 0))
    full = lambda w: pl.BlockSpec((1, t, w), lambda bi, i: (bi, 0, 0))
    return pl.pallas_call(
        functools.partial(_dsa_prompt_kernel, n_sel=n_sel),
        grid=(b, t // tq),
        in_specs=[rowq(HD), rowq(LANES), rowq(HD), full(LANES), full(HD), full(HD)],
        out_specs=rowq(HD),
        out_shape=jax.ShapeDtypeStruct((b, t, HD), BF16),
        scratch_shapes=[pltpu.VMEM((tq, t), I32),
                        pltpu.VMEM((N_HEADS, tq, LANES), F32),
                        pltpu.VMEM((N_HEADS, tq, LANES), F32),
                        pltpu.VMEM((N_HEADS, tq, LANES), F32)],
        compiler_params=_cparams(("arbitrary", "arbitrary")),
        name="dsa_prompt",
    )(qi, small, qa, kid, kab, vab)


FOX_TQ = 256
FOX_KC = 512


def _fox_prompt_kernel(qb_ref, small_ref, drow_ref, kb_ref, vb_ref, o_ref, m_scr, l_scr, acc_scr):
    qt = pl.program_id(1)
    tq, kc = FOX_TQ, FOX_KC
    nkc = ((qt + 1) * tq + kc - 1) // kc
    small = small_ref[0]
    qb = qb_ref[0]
    qpos = qt * tq + lax.broadcasted_iota(I32, (tq, kc), 0)
    lane_k = lax.broadcasted_iota(I32, (tq, kc), 1)
    m_scr[...] = jnp.full(m_scr.shape, NEG_INF, F32)
    l_scr[...] = jnp.zeros(l_scr.shape, F32)
    acc_scr[...] = jnp.zeros(acc_scr.shape, F32)
    qm = [_pair_masked(qb, h) for h in range(N_HEADS)]

    def body(c, carry):
        off = pl.multiple_of(c * kc, kc)
        causal = off + lane_k <= qpos
        for h in range(N_HEADS):
            p0 = (h // 2) * LANES
            kp = kb_ref[0, pl.ds(off, kc), p0:p0 + LANES]
            vp = vb_ref[0, pl.ds(off, kc), p0:p0 + LANES]
            bias = small[:, L_D + h:L_D + h + 1] - drow_ref[0, h:h + 1, pl.ds(off, kc)]
            s = jnp.where(causal, _dot_nt(qm[h], kp) + bias, NEG_INF)
            m_old = m_scr[h]
            m_new = jnp.maximum(m_old, jnp.max(s, axis=1, keepdims=True))
            p = jnp.exp(s - m_new[:, 0:1])
            alpha = jnp.exp(m_old - m_new)
            l_scr[h] = alpha * l_scr[h] + jnp.sum(p, axis=1, keepdims=True)
            acc_scr[h] = alpha * acc_scr[h] + _dot(p.astype(BF16), vp)
            m_scr[h] = m_new
        return carry

    lax.fori_loop(0, nkc, body, 0)

    lane = lax.broadcasted_iota(I32, (tq, LANES), 1)
    for p in range(N_HEADS // 2):
        lo = acc_scr[2 * p] / l_scr[2 * p]
        hi = acc_scr[2 * p + 1] / l_scr[2 * p + 1]
        o_ref[0, :, p * LANES:(p + 1) * LANES] = jnp.where(lane < HEAD_DIM, lo, hi).astype(o_ref.dtype)


def _fox_prompt(qb, small, drow, kbb, vbb):
    b, t, _ = qb.shape
    tq = FOX_TQ
    rowq = lambda w: pl.BlockSpec((1, tq, w), lambda bi, i: (bi, i, 0))
    full = lambda w: pl.BlockSpec((1, t, w), lambda bi, i: (bi, 0, 0))
    return pl.pallas_call(
        _fox_prompt_kernel,
        grid=(b, t // tq),
        in_specs=[rowq(HD), rowq(LANES), pl.BlockSpec((1, N_HEADS, t), lambda bi, i: (bi, 0, 0)),
                  full(HD), full(HD)],
        out_specs=rowq(HD),
        out_shape=jax.ShapeDtypeStruct((b, t, HD), BF16),
        scratch_shapes=[pltpu.VMEM((N_HEADS, tq, LANES), F32)] * 3,
        compiler_params=_cparams(("arbitrary", "arbitrary")),
        name="fox_prompt",
    )(qb, small, drow, kbb, vbb)


def _mix_kernel(x_ref, oa_ref, ob_ref, ga_ref, gb_ref, gt1_ref, sh2_ref, sc2_ref, gn2_ref,
                woa_ref, wob_ref, wout_ref, wr_ref, br_ref, x1_o, h2_o, comb_o):
    ya = _dot(oa_ref[0], woa_ref[...])
    yb = _dot(ob_ref[0], wob_ref[...])
    mix = ga_ref[0].astype(F32) * ya + gb_ref[0].astype(F32) * yb
    x1 = x_ref[0] + gt1_ref[0] * _dot(mix.astype(BF16), wout_ref[...])
    x1_o[0] = x1
    ms = jnp.mean(x1 * x1, axis=-1, keepdims=True)
    h2 = (x1 * lax.rsqrt(ms + EPS) * gn2_ref[...]) * (1.0 + sc2_ref[0]) + sh2_ref[0]
    h2b = h2.astype(BF16)
    h2_o[0] = h2b
    logits = _dot(h2b, wr_ref[...]) + br_ref[...]
    lane = lax.broadcasted_iota(I32, logits.shape, 1)
    work = logits
    vals, idxs = [], []
    for _ in range(TOP_K_EXPERTS):
        mx = jnp.max(work, axis=1, keepdims=True)
        idx = jnp.min(jnp.where(work == mx, lane, LANES), axis=1, keepdims=True)
        vals.append(mx)
        idxs.append(idx)
        work = jnp.where(lane == idx, NEG_INF, work)
    es = [jnp.exp(v - vals[0]) for v in vals]
    denom = es[0] + es[1] + es[2] + es[3]
    comb = jnp.zeros(logits.shape, F32)
    for e, idx in zip(es, idxs):
        comb = comb + jnp.where(lane == idx, e / denom, 0.0)
    comb_o[0] = comb


def _mix(x, oa, ob, ga, gb, gt1, sh2, sc2, g_norm2, mw, tm):
    woa, wob, wout, wr, br = mw
    b, t, d = x.shape
    per_row = gt1.shape[1] != 1
    mod_spec = (pl.BlockSpec((1, tm, d), lambda bi, i: (bi, i, 0)) if per_row
                else pl.BlockSpec((1, 1, d), lambda bi, i: (bi, 0, 0)))
    const = lambda shape: pl.BlockSpec(shape, lambda bi, i: tuple(0 for _ in shape))
    row = lambda width: pl.BlockSpec((1, tm, width), lambda bi, i: (bi, i, 0))
    return pl.pallas_call(
        _mix_kernel,
        grid=(b, t // tm),
        in_specs=[row(d), row(HD), row(HD), row(d), row(d), mod_spec, mod_spec, mod_spec,
                  const((1, d)), const(woa.shape), const(wob.shape), const(wout.shape),
                  const(wr.shape), const(br.shape)],
        out_specs=[row(d), row(d), row(LANES)],
        out_shape=[jax.ShapeDtypeStruct((b, t, d), F32), jax.ShapeDtypeStruct((b, t, d), BF16),
                   jax.ShapeDtypeStruct((b, t, LANES), F32)],
        compiler_params=_cparams(("arbitrary", "arbitrary")),
        name="mix_router",
    )(x, oa, ob, ga, gb, gt1, sh2, sc2, g_norm2.reshape(1, d), woa, wob, wout, wr, br)


def _moe_kernel(h_ref, comb_ref, x1_ref, gt2_ref, wup_ref, bup_ref, wdn_ref, bdn_ref, y_o, acc_ref):
    e = pl.program_id(2)

    @pl.when(e == 0)
    def _():
        acc_ref[...] = jnp.zeros_like(acc_ref)

    hu = _dot(h_ref[0], wup_ref[0]) + bup_ref[0]
    g = jnp.minimum(hu[:, :D_FF], SWIGLU_LIMIT)
    lin = jnp.clip(hu[:, D_FF:], -SWIGLU_LIMIT, SWIGLU_LIMIT)
    act = (lin + 1.0) * g * _sigmoid(SWIGLU_ALPHA * g)
    ye = _dot(act.astype(BF16), wdn_ref[0]) + bdn_ref[0]
    comb = comb_ref[0]
    lane = lax.broadcasted_iota(I32, comb.shape, 1)
    ce = jnp.sum(jnp.where(lane == e, comb, 0.0), axis=1, keepdims=True)
    acc_ref[...] += ce * ye

    @pl.when(e == pl.num_programs(2) - 1)
    def _():
        y_o[0] = x1_ref[0] + gt2_ref[0] * acc_ref[...]


def _moe(h2, comb, x1, gt2, ew, tm):
    wup, bup, wdn, bdn = ew
    b, t, d = x1.shape
    ne = wup.shape[0]
    per_row = gt2.shape[1] != 1
    mod_spec = (pl.BlockSpec((1, tm, d), lambda bi, i, e: (bi, i, 0)) if per_row
                else pl.BlockSpec((1, 1, d), lambda bi, i, e: (bi, 0, 0)))
    row = lambda width: pl.BlockSpec((1, tm, width), lambda bi, i, e: (bi, i, 0))
    return pl.pallas_call(
        _moe_kernel,
        grid=(b, t // tm, ne),
        in_specs=[row(d), row(LANES), row(d), mod_spec,
                  pl.BlockSpec((1, d, 2 * D_FF), lambda bi, i, e: (e, 0, 0)),
                  pl.BlockSpec((1, 1, 2 * D_FF), lambda bi, i, e: (e, 0, 0)),
                  pl.BlockSpec((1, D_FF, d), lambda bi, i, e: (e, 0, 0)),
                  pl.BlockSpec((1, 1, d), lambda bi, i, e: (e, 0, 0))],
        out_specs=row(d),
        out_shape=jax.ShapeDtypeStruct((b, t, d), F32),
        scratch_shapes=[pltpu.VMEM((tm, d), F32)],
        compiler_params=_cparams(("arbitrary", "arbitrary", "arbitrary")),
        name="moe",
    )(h2, comb, x1, gt2, wup, bup, wdn, bdn)


def _mod_split(mod, lo, hi):
    return jnp.split(mod[lo:hi], N_ADA, axis=-1)


def _prompt_group(x, mods, pw, mw, ew, g_norm1, g_norm2):
    b, t, d = x.shape
    sh1, sc1, gt1, sh2, sc2, gt2 = [m[:, None, :] for m in mods]
    tables = _rope_tables(jnp.arange(t, dtype=I32))
    (ka, va, kb, vb, small, qa, qi, qb, kab, vab, kbb, vbb, kid, ga, gb) = _proj(
        x, g_norm1, sh1, sc1, pw, tables, tm=min(256, t))
    out_a = _dsa_prompt(qi, small, qa, kid, kab, vab)
    drow = jnp.swapaxes(small[:, :, L_D:L_D + N_HEADS], 1, 2)
    out_b = _fox_prompt(qb, small, drow, kbb, vbb)
    x1, h2, comb = _mix(x, out_a, out_b, ga, gb, gt1, sh2, sc2, g_norm2, mw, tm=min(256, t))
    y = _moe(h2, comb, x1, gt2, ew, tm=min(512, t))
    heads = lambda z: z.reshape(1, b, t, N_HEADS, HEAD_DIM)
    return (y, heads(ka), heads(va), small[None, :, :, :D_IDX], heads(kb), heads(vb),
            small[None, :, :, L_LF:L_D])


def _mix_weights(w_oa, w_ob, w_out, w_router, b_router):
    ne = w_router.shape[1]
    wr = jnp.concatenate([w_router, jnp.zeros((w_router.shape[0], LANES - ne), w_router.dtype)], axis=1)
    br = jnp.concatenate([b_router, jnp.full((LANES - ne,), NEG_INF, b_router.dtype)]).reshape(1, LANES)
    return w_oa.astype(BF16), w_ob.astype(BF16), w_out.astype(BF16), wr.astype(BF16), br


def _moe_weights(w_up, b_up, w_down, b_down):
    ne = w_up.shape[0]
    return (w_up.astype(BF16), b_up.reshape(ne, 1, -1), w_down.astype(BF16), b_down.reshape(ne, 1, -1))


SCORE_CHUNK = 1024


def _page_gather(pt_ref, b, cache_ref, dst_slab, sem):
    npages = pt_ref.shape[1]

    def copy(p, page):
        return pltpu.make_async_copy(cache_ref.at[page], dst_slab(p), sem)

    def issue(p, c):
        copy(p, pt_ref[b, p]).start()
        return c

    def wait(p, c):
        copy(p, 0).wait()
        return c

    lax.fori_loop(0, npages, issue, 0)
    lax.fori_loop(0, npages, wait, 0)


def _sidx_kernel(pt_ref, q8_ref, w8_ref, knew_ref, cache_ref, sc_ref, kbuf, sem):
    b = pl.program_id(0)
    _page_gather(pt_ref, b, cache_ref, lambda p: kbuf.at[pl.ds(p * PAGE_SIZE, PAGE_SIZE)], sem)
    n_past = kbuf.shape[0]
    q8 = q8_ref[0]
    w8 = w8_ref[0]
    for c in range(n_past // SCORE_CHUNK):
        kb = kbuf[c * SCORE_CHUNK:(c + 1) * SCORE_CHUNK, :].astype(BF16)
        d = _dot_nt(q8, kb)
        sc_ref[0, :, c * SCORE_CHUNK:(c + 1) * SCORE_CHUNK] = jnp.sum(
            w8 * jnp.maximum(d, 0.0), axis=0, keepdims=True)
    dn = jnp.sum(q8.astype(F32) * knew_ref[0], axis=1, keepdims=True)
    sn = jnp.sum(w8 * jnp.maximum(dn, 0.0), axis=0, keepdims=True)
    lane = lax.broadcasted_iota(I32, (1, LANES), 1)
    sc_ref[0, :, n_past:n_past + LANES] = jnp.where(lane == 0, sn, NEG_INF)


def _sample_scores(page_table, q8, w8, knew, cache_kidx):
    nb, npages = page_table.shape
    n_past = npages * PAGE_SIZE
    grid_spec = pltpu.PrefetchScalarGridSpec(
        num_scalar_prefetch=1,
        grid=(nb,),
        in_specs=[pl.BlockSpec((1, N_HEADS, D_IDX), lambda b, pt: (b, 0, 0)),
                  pl.BlockSpec((1, N_HEADS, 1), lambda b, pt: (b, 0, 0)),
                  pl.BlockSpec((1, 1, D_IDX), lambda b, pt: (b, 0, 0)),
                  pl.BlockSpec(memory_space=pl.ANY)],
        out_specs=pl.BlockSpec((1, 1, n_past + LANES), lambda b, pt: (b, 0, 0)),
        scratch_shapes=[pltpu.VMEM((n_past, D_IDX), F32), pltpu.SemaphoreType.DMA(())],
    )
    return pl.pallas_call(
        _sidx_kernel,
        grid_spec=grid_spec,
        out_shape=jax.ShapeDtypeStruct((nb, 1, n_past + LANES), F32),
        compiler_params=_cparams(("arbitrary",)),
        name="sample_scores",
    )(page_table, q8, w8, knew, cache_kidx)


def _ssel_kernel(sc_ref, o_ref, key_scr, *, n_sel):
    rows, n = sc_ref.shape
    nch = n // LANES
    key_scr[...] = _sort_key(sc_ref[...] + 0.0)

    def count(pred):
        def body(c, cnt):
            off = pl.multiple_of(c * LANES, LANES)
            return cnt + pred(key_scr[:, pl.ds(off, LANES)]).astype(I32)
        cnt = lax.fori_loop(0, nch, body, jnp.zeros((rows, LANES), I32))
        return jnp.sum(cnt, axis=1, keepdims=True)

    thr = _topk_threshold(lambda cand: count(lambda k: k >= cand), rows, n_sel)
    cnt_gt = count(lambda k: k > thr)
    cnt_ge = count(lambda k: k >= thr)
    need = n_sel - cnt_gt
    tie = (cnt_ge > n_sel) & (thr > KEY_NEG_INF)

    @pl.when(jnp.max(tie.astype(I32)) > 0)
    def _():
        r_i = lax.broadcasted_iota(I32, (LANES, LANES), 0)
        c_i = lax.broadcasted_iota(I32, (LANES, LANES), 1)
        upper = (r_i < c_i).astype(BF16)

        def body(c, seen):
            off = pl.multiple_of(c * LANES, LANES)
            k = key_scr[:, pl.ds(off, LANES)]
            eq = k == thr
            eqf = jnp.where(eq, 1.0, 0.0)
            rank = seen + _dot(eqf.astype(BF16), upper)
            demote = eq & tie & (rank >= need.astype(F32))
            key_scr[:, pl.ds(off, LANES)] = jnp.where(demote, KEY_NEG_INF, k)
            return seen + jnp.sum(eqf, axis=1, keepdims=True)

        lax.fori_loop(0, nch, body, jnp.zeros((rows, 1), F32))

    k = key_scr[...]
    o_ref[...] = jnp.where((k >= thr) & (k > KEY_NEG_INF), 0.0, NEG_INF)


def _sample_select(scores, n_keys):
    rows, n = scores.shape
    n_sel = min(TOPK_MAX, n_keys // 4)
    return pl.pallas_call(
        functools.partial(_ssel_kernel, n_sel=n_sel),
        out_shape=jax.ShapeDtypeStruct((rows, n), F32),
        scratch_shapes=[pltpu.VMEM((rows, n), I32)],
        compiler_params=pltpu.CompilerParams(vmem_limit_bytes=VMEM_LIMIT),
        name="sample_select",
    )(scores)


TOK_PER_ROW = LANES // N_HEADS
ROWS_PER_PAGE = PAGE_SIZE // TOK_PER_ROW


def _split3(x):
    hi = x.astype(BF16)
    r1 = x - hi.astype(F32)
    mid = r1.astype(BF16)
    lo = (r1 - mid.astype(F32)).astype(BF16)
    return hi, mid, lo


def _dot3_l(x, m):
    hi, mid, lo = _split3(x)
    return _dot(hi, m) + _dot(mid, m) + _dot(lo, m)


def _dot3_r(m, x):
    hi, mid, lo = _split3(x)
    return _dot(m, hi) + _dot(m, mid) + _dot(m, lo)


def _fbias_kernel(pt_ref, lfnew_ref, msuf_ref, cache_ref, o_ref, lbuf, sem):
    b = pl.program_id(0)
    _page_gather(pt_ref, b, cache_ref, lambda p: lbuf.at[p], sem)
    npages = lbuf.shape[0]
    r_i = lax.broadcasted_iota(I32, (npages, npages), 0)
    c_i = lax.broadcasted_iota(I32, (npages, npages), 1)
    later_pages = (c_i > r_i).astype(BF16)
    strict = jnp.zeros((npages, N_HEADS * PAGE_SIZE), F32)
    for r in range(ROWS_PER_PAGE):
        strict = strict + _dot3_l(lbuf[:, r, :], msuf_ref[r])
    x0 = lbuf[:, 0, :]
    lfnew = lfnew_ref[0]
    for h in range(N_HEADS):
        st = strict[:, h * PAGE_SIZE:(h + 1) * PAGE_SIZE]
        tot = st[:, 0:1] + x0[:, h:h + 1]
        later = _dot3_r(later_pages, jnp.broadcast_to(tot, (npages, PAGE_SIZE)))
        o_ref[0, h] = st + later + lfnew[:, h:h + 1]


def _suffix_matrices():
    r = np.arange(ROWS_PER_PAGE)[:, None, None]
    j = np.arange(LANES)[None, :, None]
    col = np.arange(N_HEADS * PAGE_SIZE)[None, None, :]
    h, tok = col // PAGE_SIZE, col % PAGE_SIZE
    m = ((j % N_HEADS) == h) & ((r * TOK_PER_ROW + j // N_HEADS) > tok)
    return jnp.asarray(m, BF16)


def _sample_fox_bias(page_table, lfnew, cache_logf):
    nb, npages = page_table.shape
    msuf = _suffix_matrices()
    grid_spec = pltpu.PrefetchScalarGridSpec(
        num_scalar_prefetch=1,
        grid=(nb,),
        in_specs=[pl.BlockSpec((1, 1, N_HEADS), lambda b, pt: (b, 0, 0)),
                  pl.BlockSpec(msuf.shape, lambda b, pt: (0, 0, 0)),
                  pl.BlockSpec(memory_space=pl.ANY)],
        out_specs=pl.BlockSpec((1, N_HEADS, npages, PAGE_SIZE), lambda b, pt: (b, 0, 0, 0)),
        scratch_shapes=[pltpu.VMEM((npages, ROWS_PER_PAGE, LANES), F32), pltpu.SemaphoreType.DMA(())],
    )
    return pl.pallas_call(
        _fbias_kernel,
        grid_spec=grid_spec,
        out_shape=jax.ShapeDtypeStruct((nb, N_HEADS, npages, PAGE_SIZE), F32),
        compiler_params=_cparams(("arbitrary",)),
        name="sample_fox_bias",
    )(page_table, lfnew, msuf, cache_logf)


DEC_PAGES = 8
QROWS = 16


def _decode_kernel(pt_ref, q_ref, knew_ref, vnew_ref, bnew_ref, *refs):
    npg = DEC_PAGES
    k_refs, v_refs, b_refs = refs[:npg], refs[npg:2 * npg], refs[2 * npg:3 * npg]
    o_ref, m_scr, l_scr, acc_scr = refs[3 * npg:]
    s = pl.program_id(1)
    hrow = lax.broadcasted_iota(I32, (QROWS, HD), 0)
    lane = lax.broadcasted_iota(I32, (QROWS, HD), 1)
    own = (lane // HEAD_DIM) == hrow
    qf = jnp.where(own, jnp.broadcast_to(q_ref[0].astype(F32), (QROWS, HD)), 0.0)
    qm = qf.astype(BF16)

    @pl.when(s == 0)
    def _():
        sn = jnp.sum(qf[:N_HEADS] * knew_ref[0], axis=1, keepdims=True) + bnew_ref[0]
        live = sn > NEG_INF
        m_scr[...] = jnp.broadcast_to(sn, m_scr.shape)
        l_scr[...] = jnp.broadcast_to(jnp.where(live, 1.0, 0.0), l_scr.shape)
        acc_scr[...] = jnp.where(live, 1.0, 0.0) * jnp.broadcast_to(vnew_ref[0], acc_scr.shape)

    for j in range(npg):
        kp = k_refs[j][0].astype(BF16)
        vp = v_refs[j][0].astype(BF16)
        sc = _dot_nt(qm, kp)[:N_HEADS] + b_refs[j][0, 0]
        m_old = m_scr[...]
        m_new = jnp.maximum(m_old, jnp.max(sc, axis=1, keepdims=True))
        m_safe = jnp.where(m_new == NEG_INF, 0.0, m_new)
        p = jnp.exp(sc - m_safe[:, 0:1])
        alpha = jnp.exp(m_old - m_safe)
        l_scr[...] = alpha * l_scr[...] + jnp.sum(p, axis=1, keepdims=True)
        p16 = jnp.concatenate([p, jnp.zeros_like(p)], axis=0).astype(BF16)
        acc_scr[...] = alpha[:, 0:1] * acc_scr[...] + _dot(p16, vp)[:N_HEADS]
        m_scr[...] = m_new

    @pl.when(s == pl.num_programs(1) - 1)
    def _():
        o = jnp.where(own[:N_HEADS], acc_scr[...] / l_scr[:, 0:1], 0.0)
        o_ref[0] = jnp.sum(o, axis=0, keepdims=True)


def _paged_decode(page_table, q, knew, vnew, bnew, cache_k, cache_v, bias):
    nb, npages = page_table.shape
    hb = bias.shape[2]
    npg = DEC_PAGES
    kv_spec = lambda j: pl.BlockSpec((1, PAGE_SIZE, HD), lambda b, s, pt: (pt[b, s * npg + j], 0, 0))
    b_spec = lambda j: pl.BlockSpec((1, 1, hb, PAGE_SIZE), lambda b, s, pt: (b, s * npg + j, 0, 0))
    row = lambda w: pl.BlockSpec((1, 1, w), lambda b, s, pt: (b, 0, 0))
    grid_spec = pltpu.PrefetchScalarGridSpec(
        num_scalar_prefetch=1,
        grid=(nb, npages // npg),
        in_specs=[row(HD), row(HD), row(HD), pl.BlockSpec((1, N_HEADS, 1), lambda b, s, pt: (b, 0, 0))]
                 + [kv_spec(j) for j in range(npg)] * 2 + [b_spec(j) for j in range(npg)],
        out_specs=row(HD),
        scratch_shapes=[pltpu.VMEM((N_HEADS, LANES), F32), pltpu.VMEM((N_HEADS, LANES), F32),
                        pltpu.VMEM((N_HEADS, HD), F32)],
    )
    return pl.pallas_call(
        _decode_kernel,
        grid_spec=grid_spec,
        out_shape=jax.ShapeDtypeStruct((nb, 1, HD), F32),
        compiler_params=_cparams(("arbitrary", "arbitrary")),
        name="paged_decode",
    )(page_table, q, knew, vnew, bnew, *([cache_k] * npg), *([cache_v] * npg), *([bias] * npg))


def _sample_group(x, mods, pw, mw, ew, g_norm1, g_norm2, caches, page_table, pos):
    nb, t, d = x.shape
    cache_k_a, cache_v_a, cache_kidx, cache_k_b, cache_v_b, cache_logf = caches
    n_pool = cache_k_a.shape[0]
    npages = page_table.shape[1]
    n_past = npages * PAGE_SIZE
    sh1, sc1, gt1, sh2, sc2, gt2 = [m[None] for m in mods]
    xs = x.reshape(1, nb, d)
    tables = _rope_tables(jnp.full((nb,), pos, I32))
    (ka, va, kb, vb, small, qa, qi, qb, _, _, _, _, _, ga, gb) = _proj(
        xs, g_norm1, sh1, sc1, pw, tables, tm=nb)
    col = lambda z: z.reshape(nb, 1, -1)
    scores = _sample_scores(page_table, qi.reshape(nb, N_HEADS, D_IDX),
                            small[0, :, L_WI:L_LF].reshape(nb, N_HEADS, 1),
                            small[0, :, :D_IDX].reshape(nb, 1, D_IDX), cache_kidx)
    mask = _sample_select(scores.reshape(nb, -1), n_past + t)
    bias_a = mask[:, :n_past].reshape(nb, npages, 1, PAGE_SIZE)
    bnew_a = jnp.broadcast_to(mask[:, n_past:n_past + 1, None], (nb, N_HEADS, 1))
    out_a = _paged_decode(page_table, col(qa), col(ka), col(va), bnew_a,
                          cache_k_a.reshape(n_pool, PAGE_SIZE, HD), cache_v_a.reshape(n_pool, PAGE_SIZE, HD),
                          bias_a)
    logf = small[0, :, L_LF:L_D]
    bias_b = _sample_fox_bias(page_table, logf.reshape(nb, 1, N_HEADS),
                              cache_logf.reshape(n_pool, ROWS_PER_PAGE, LANES))
    out_b = _paged_decode(page_table, col(qb), col(kb), col(vb), jnp.zeros((nb, N_HEADS, 1), F32),
                          cache_k_b.reshape(n_pool, PAGE_SIZE, HD), cache_v_b.reshape(n_pool, PAGE_SIZE, HD),
                          jnp.swapaxes(bias_b, 1, 2))
    oa = out_a.reshape(1, nb, HD).astype(BF16)
    ob = out_b.reshape(1, nb, HD).astype(BF16)
    x1, h2, comb = _mix(xs, oa, ob, ga, gb, gt1, sh2, sc2, g_norm2, mw, tm=nb)
    y = _moe(h2, comb, x1, gt2, ew, tm=nb)
    heads = lambda z: z.reshape(1, nb, t, N_HEADS, HEAD_DIM)
    return (y.reshape(nb, t, d), heads(ka), heads(va), small[0, :, :D_IDX].reshape(1, nb, t, D_IDX),
            heads(kb), heads(vb), logf.reshape(1, nb, t, N_HEADS))


def kernel(x_prompt, x_sample, c_prompt, c_sample, cache_k_a, cache_v_a, cache_kidx_a, cache_k_b,
           cache_v_b, cache_logf_b, page_table, w_ada, b_ada, g_norm1, w_in, b_f, g_qa, g_ka, g_kidx,
           g_qb, g_kb, w_oa, w_ob, w_out, g_norm2, w_router, b_router, w_up, b_up, w_down, b_down):
    depth = w_in.shape[0]
    assert depth == 1, "single-layer trunk"
    l = 0
    nbp, nbs = x_prompt.shape[0], x_sample.shape[0]
    n_past = page_table.shape[1] * PAGE_SIZE
    rows = nbp + nbs
    pad = (-rows) % 8
    c_all = jnp.concatenate([c_prompt, c_sample, jnp.zeros((pad, c_prompt.shape[1]), F32)], axis=0)
    mod = _adaln(c_all, w_ada[l], b_ada[l])
    pw = _proj_weights(w_in[l], b_f[l], g_qa[l], g_ka[l], g_kidx[l], g_qb[l], g_kb[l])
    mw = _mix_weights(w_oa[l], w_ob[l], w_out[l], w_router[l], b_router[l])
    ew = _moe_weights(w_up[l], b_up[l], w_down[l], b_down[l])
    outs_p = _prompt_group(x_prompt, _mod_split(mod, 0, nbp), pw, mw, ew, g_norm1[l], g_norm2[l])
    caches = (cache_k_a[l], cache_v_a[l], cache_kidx_a[l], cache_k_b[l], cache_v_b[l], cache_logf_b[l])
    outs_s = _sample_group(x_sample, _mod_split(mod, nbp, rows), pw, mw, ew, g_norm1[l], g_norm2[l],
                           caches, page_table, n_past)
    return (outs_p[0], outs_s[0]) + tuple(outs_p[1:]) + tuple(outs_s[1:])
```

```python
import functools

import numpy as np
import jax
import jax.numpy as jnp
from jax import lax
from jax.experimental import pallas as pl
from jax.experimental.pallas import tpu as pltpu

F32 = jnp.float32
BF16 = jnp.bfloat16
I32 = jnp.int32

D_MODEL = 1024
HEAD_DIM = 64
N_HEADS = 8
HD = N_HEADS * HEAD_DIM
D_IDX = 64
ROT_DIM = HEAD_DIM // 4
ROPE_THETA = 500000.0
TOPK_MAX = 256
N_EXPERTS = 32
TOP_K_EXPERTS = 4
D_FF = D_MODEL
SWIGLU_LIMIT = 7.0
SWIGLU_ALPHA = 1.702
EPS = 1e-6
N_ADA = 6
PAGE_SIZE = 128
LANES = 128
W_IDX_SCALE = N_HEADS ** -0.5 * D_IDX ** -0.5
QK_SCALE = HEAD_DIM ** -0.5

C_QA, C_KA, C_VA, C_QI, C_QB, C_KB, C_VB = (i * HD for i in range(7))
C_GA = 7 * HD
C_GB = C_GA + D_MODEL
C_SM = C_GB + D_MODEL
D_IN_PAD = C_SM + LANES
L_WI = D_IDX
L_LF = D_IDX + N_HEADS
L_D = L_LF + N_HEADS

NEG_INF = float("-inf")
INT_MIN = -2 ** 31
KEY_NEG_INF = int(np.int32(np.uint32(0xFF800000) ^ np.uint32(0x7FFFFFFF)))

VMEM_LIMIT = 56 * 1024 * 1024


def _cparams(sem):
    return pltpu.CompilerParams(dimension_semantics=sem, vmem_limit_bytes=VMEM_LIMIT)


def _sigmoid(x):
    return 1.0 / (1.0 + jnp.exp(-x))


def _dot(a, b):
    return jnp.dot(a, b, preferred_element_type=F32)


def _dot_nt(a, b):
    return lax.dot_general(a, b, (((1,), (1,)), ((), ())), preferred_element_type=F32)


def _sort_key(x):
    b = pltpu.bitcast(x, I32)
    return b ^ ((b >> 31) & jnp.int32(0x7FFFFFFF))


def _adaln_kernel(c_ref, w_ref, b_ref, o_ref):
    c = c_ref[...]
    s = c * _sigmoid(c)
    o_ref[...] = _dot(s.astype(BF16), w_ref[...].astype(BF16)) + b_ref[...]


def _adaln(c, w_ada, b_ada):
    rows, d = c.shape
    n = w_ada.shape[1]
    tn = 1536
    return pl.pallas_call(
        _adaln_kernel,
        grid=(n // tn,),
        in_specs=[pl.BlockSpec((rows, d), lambda j: (0, 0)),
                  pl.BlockSpec((d, tn), lambda j: (0, j)),
                  pl.BlockSpec((1, tn), lambda j: (0, j))],
        out_specs=pl.BlockSpec((rows, tn), lambda j: (0, j)),
        out_shape=jax.ShapeDtypeStruct((rows, n), F32),
        compiler_params=_cparams(("arbitrary",)),
        name="adaln",
    )(c, w_ada, b_ada.reshape(1, n))


def _rope(z, cos, sp, sm):
    w = z.shape[1]
    return z * cos + pltpu.roll(z, 8, 1) * sp + pltpu.roll(z, w - 8, 1) * sm


def _tile4(t):
    return jnp.concatenate([t, t, t, t], axis=1)


def _proj_kernel(x_ref, gn_ref, sh_ref, sc_ref, w_ref, gains_ref, gsm_ref, bf_ref,
                 cos_ref, sp_ref, sm_ref, gmat_ref, gsmat_ref,
                 ka_o, va_o, kb_o, vb_o, small_o,
                 qa_o, qi_o, qb_o, kab_o, vab_o, kbb_o, vbb_o, kid_o, ga_o, gb_o,
                 carry_ref):
    i = pl.program_id(1)
    tm = x_ref.shape[1]

    @pl.when(i == 0)
    def _():
        carry_ref[...] = jnp.zeros_like(carry_ref)

    x = x_ref[0]
    ms = jnp.mean(x * x, axis=-1, keepdims=True)
    y = x * lax.rsqrt(ms + EPS) * gn_ref[...]
    h = y * (1.0 + sc_ref[0]) + sh_ref[0]
    hb = h.astype(BF16)

    cos128, sp128, sm128 = cos_ref[...], sp_ref[...], sm_ref[...]
    cos, sp, sm = _tile4(cos128), _tile4(sp128), _tile4(sm128)
    gmat = gmat_ref[...]

    def sect(c0, width=HD):
        return _dot(hb, w_ref[:, c0:c0 + width])

    def headnorm(z, g):
        msq = _dot((z * z).astype(BF16), gmat)
        return z * lax.rsqrt(msq + EPS) * g

    qa = _rope(headnorm(sect(C_QA), gains_ref[0:1, :]), cos, sp, sm)
    qa_o[0] = (qa * QK_SCALE).astype(BF16)
    ka = _rope(headnorm(sect(C_KA), gains_ref[1:2, :]), cos, sp, sm)
    ka_o[0] = ka
    kab_o[0] = ka.astype(BF16)
    va = sect(C_VA)
    va_o[0] = va
    vab_o[0] = va.astype(BF16)
    qi_o[0] = _rope(sect(C_QI), cos, sp, sm).astype(BF16)
    qb_o[0] = (headnorm(sect(C_QB), gains_ref[2:3, :]) * QK_SCALE).astype(BF16)
    kb = headnorm(sect(C_KB), gains_ref[3:4, :])
    kb_o[0] = kb
    kbb_o[0] = kb.astype(BF16)
    vb = sect(C_VB)
    vb_o[0] = vb
    vbb_o[0] = vb.astype(BF16)
    ga_o[0] = _sigmoid(sect(C_GA, D_MODEL)).astype(BF16)
    gb_o[0] = _sigmoid(sect(C_GB, D_MODEL)).astype(BF16)

    zs = sect(C_SM, LANES)
    lane = lax.broadcasted_iota(I32, (tm, LANES), 1)
    msq = _dot((zs * zs).astype(BF16), gsmat_ref[...])
    ki = _rope(zs * lax.rsqrt(msq + EPS) * gsm_ref[...], cos128, sp128, sm128)
    kid_o[0] = (ki + pltpu.roll(ki, D_IDX, 1)).astype(BF16)
    t = zs + bf_ref[...]
    logf = jnp.minimum(t, 0.0) - jnp.log1p(jnp.exp(-jnp.abs(t)))
    lf_only = jnp.where((lane >= L_LF) & (lane < L_D), logf, 0.0)
    r_i = lax.broadcasted_iota(I32, (tm, tm), 0)
    c_i = lax.broadcasted_iota(I32, (tm, tm), 1)
    tri = (c_i <= r_i).astype(F32)
    dcum = jnp.dot(tri, lf_only, preferred_element_type=F32,
                   precision=lax.Precision.HIGHEST) + carry_ref[...]
    carry_ref[...] = dcum[tm - 1:tm, :]
    small = (ki + jnp.where((lane >= L_WI) & (lane < L_LF), zs * W_IDX_SCALE, 0.0)
             + lf_only + pltpu.roll(dcum, N_HEADS, 1))
    small_o[0] = small


def _rope_tables(pos):
    half = ROT_DIM // 2
    inv_freq = ROPE_THETA ** (-jnp.arange(half, dtype=F32) * 2.0 / ROT_DIM)
    ang = pos.astype(F32)[:, None] * inv_freq[None, :]
    cos8, sin8 = jnp.cos(ang), jnp.sin(ang)
    t = pos.shape[0]
    one = jnp.ones((t, HEAD_DIM - ROT_DIM), F32)
    zero = jnp.zeros((t, HEAD_DIM - ROT_DIM), F32)
    z8 = jnp.zeros((t, half), F32)
    cos64 = jnp.concatenate([cos8, cos8, one], axis=1)
    sp64 = jnp.concatenate([z8, sin8, zero], axis=1)
    sm64 = jnp.concatenate([-sin8, z8, zero], axis=1)
    dup = lambda a: jnp.concatenate([a, a], axis=1)
    return dup(cos64), dup(sp64), dup(sm64)


def _proj_weights(w_in, b_f, g_qa, g_ka, g_kidx, g_qb, g_kb):
    offs = np.cumsum((HD, HD, HD, HD, D_IDX, N_HEADS, HD, HD, HD, N_HEADS, D_MODEL, D_MODEL))[:-1].tolist()
    qa, ka, va, qi, ki, wi, qb, kb, vb, fb, ga, gb = jnp.split(w_in, offs, axis=1)
    pad = jnp.zeros((w_in.shape[0], LANES - D_IDX - 2 * N_HEADS), w_in.dtype)
    w = jnp.concatenate([qa, ka, va, qi, qb, kb, vb, ga, gb, ki, wi, fb, pad], axis=1).astype(BF16)
    tile8 = lambda g: jnp.tile(g, N_HEADS)
    gains = jnp.stack([tile8(g_qa), tile8(g_ka), tile8(g_qb), tile8(g_kb)]
                      + [jnp.zeros((HD,), F32)] * 4)
    gsm = jnp.concatenate([g_kidx, jnp.zeros((LANES - D_IDX,), F32)]).reshape(1, LANES)
    bfv = jnp.zeros((LANES,), F32).at[L_LF:L_D].set(b_f).reshape(1, LANES)
    blk = np.kron(np.eye(N_HEADS), np.full((HEAD_DIM, HEAD_DIM), 1.0 / HEAD_DIM))
    gmat = jnp.asarray(blk, BF16)
    gs = np.zeros((LANES, LANES))
    gs[:D_IDX, :D_IDX] = 1.0 / D_IDX
    return w, gains, gsm, bfv, gmat, jnp.asarray(gs, BF16)


def _proj(x, g_norm, shift, scale, pw, tables, tm):
    w, gains, gsm, bfv, gmat, gsmat = pw
    b, t, d = x.shape
    per_row = shift.shape[1] != 1
    mod_spec = (pl.BlockSpec((1, tm, d), lambda bi, i: (bi, i, 0)) if per_row
                else pl.BlockSpec((1, 1, d), lambda bi, i: (bi, 0, 0)))
    const = lambda shape: pl.BlockSpec(shape, lambda bi, i: tuple(0 for _ in shape))
    tab_spec = pl.BlockSpec((tm, LANES), lambda bi, i: (i, 0))
    row = lambda width: pl.BlockSpec((1, tm, width), lambda bi, i: (bi, i, 0))
    f = lambda width, dt: jax.ShapeDtypeStruct((b, t, width), dt)
    return pl.pallas_call(
        _proj_kernel,
        grid=(b, t // tm),
        in_specs=[row(d), const((1, d)), mod_spec, mod_spec, const(w.shape), const(gains.shape),
                  const((1, LANES)), const((1, LANES)), tab_spec, tab_spec, tab_spec,
                  const(gmat.shape), const(gsmat.shape)],
        out_specs=[row(HD)] * 4 + [row(LANES)] + [row(HD)] * 7 + [row(LANES)] + [row(d)] * 2,
        out_shape=[f(HD, F32)] * 4 + [f(LANES, F32)] + [f(HD, BF16)] * 7 + [f(LANES, BF16)]
                  + [f(d, BF16)] * 2,
        scratch_shapes=[pltpu.VMEM((1, LANES), F32)],
        compiler_params=_cparams(("arbitrary", "arbitrary")),
        name="proj",
    )(x, g_norm.reshape(1, d), shift, scale, w, gains, gsm, bfv, *tables, gmat, gsmat)


DSA_TQ = 256
DSA_KC = 512
DSA_SC = 256
HEAD_GROUP = 4


def _pair_masked(q, h):
    pair = q[:, (h // 2) * LANES:(h // 2 + 1) * LANES].astype(F32)
    lane = lax.broadcasted_iota(I32, pair.shape, 1)
    keep = (lane < HEAD_DIM) if h % 2 == 0 else (lane >= HEAD_DIM)
    return jnp.where(keep, pair, 0.0).astype(BF16)


def _topk_threshold(count_ge, rows, k):
    zero = jnp.zeros((rows, 1), I32)
    t0 = jnp.where(count_ge(zero) >= k, zero, jnp.full((rows, 1), INT_MIN, I32))

    def body(j, t):
        cand = t | (jnp.int32(1) << (30 - j))
        return jnp.where(count_ge(cand) >= k, cand, t)

    return lax.fori_loop(0, 31, body, t0)


def _fold_lanes(m):
    out = m[:, 0:LANES]
    for j in range(1, m.shape[1] // LANES):
        out = out + m[:, j * LANES:(j + 1) * LANES]
    return out


def _two_pass_attention(nkc, kc, heads, qm, k_chunk, v_chunk, bias_chunk, s_scr):
    tq = qm[0].shape[0]
    ng = len(heads)

    def logits_body(c, mxs):
        off = pl.multiple_of(c * kc, kc)
        out = []
        for g, h in enumerate(heads):
            s = _dot_nt(qm[g], k_chunk(h, off)) + bias_chunk(h, off)
            s_scr[g, :, pl.ds(off, kc)] = s
            mx = mxs[g]
            for j in range(kc // LANES):
                mx = jnp.maximum(mx, s[:, j * LANES:(j + 1) * LANES])
            out.append(mx)
        return tuple(out)

    neg = jnp.full((tq, LANES), NEG_INF, F32)
    mxs = lax.fori_loop(0, nkc, logits_body, (neg,) * ng)
    ms = []
    for mx in mxs:
        m = jnp.max(mx, axis=1, keepdims=True)
        ms.append(jnp.where(m == NEG_INF, 0.0, m))

    def pv_body(c, carry):
        off = pl.multiple_of(c * kc, kc)
        out = []
        for g, h in enumerate(heads):
            l, acc = carry[g]
            p = jnp.exp(s_scr[g, :, pl.ds(off, kc)] - ms[g])
            out.append((l + _fold_lanes(p), acc + _dot(p.astype(BF16), v_chunk(h, off))))
        return tuple(out)

    zero = jnp.zeros((tq, LANES), F32)
    res = lax.fori_loop(0, nkc, pv_body, ((zero, zero),) * ng)
    return [acc / jnp.sum(l, axis=1, keepdims=True) for l, acc in res]


def _store_pairs(o_ref, outs):
    lane = lax.broadcasted_iota(I32, outs[0].shape, 1)
    for p in range(N_HEADS // 2):
        o_ref[0, :, p * LANES:(p + 1) * LANES] = jnp.where(
            lane < HEAD_DIM, outs[2 * p], outs[2 * p + 1]).astype(o_ref.dtype)


def _dsa_prompt_kernel(qi_ref, small_ref, qa_ref, kid_ref, ka_ref, va_ref, o_ref,
                       key_scr, s_scr, *, n_sel):
    qt = pl.program_id(1)
    tq = qi_ref.shape[1]
    kc, sc = DSA_KC, DSA_SC
    nkc = ((qt + 1) * tq + kc - 1) // kc
    small = small_ref[0]
    qi = qi_ref[0]
    qa = qa_ref[0]
    qpos = qt * tq + lax.broadcasted_iota(I32, (tq, sc), 0)
    lane_k = lax.broadcasted_iota(I32, (tq, sc), 1)
    qim = [_pair_masked(qi, h) for h in range(N_HEADS)]

    def score_body(c, carry):
        off = pl.multiple_of(c * sc, sc)
        kid = kid_ref[0, pl.ds(off, sc), :]
        acc = jnp.zeros((tq, sc), F32)
        for h in range(N_HEADS):
            d = _dot_nt(qim[h], kid)
            acc = acc + small[:, L_WI + h:L_WI + h + 1] * jnp.maximum(d, 0.0)
        acc = jnp.where(off + lane_k <= qpos, acc + 0.0, NEG_INF)
        key_scr[:, pl.ds(off, sc)] = _sort_key(acc)
        return carry

    lax.fori_loop(0, nkc * (kc // sc), score_body, 0)

    def count(pred):
        def body(c, cnt):
            off = pl.multiple_of(c * kc, kc)
            return cnt + _fold_lanes(pred(key_scr[:, pl.ds(off, kc)]).astype(I32))
        cnt = lax.fori_loop(0, nkc, body, jnp.zeros((tq, LANES), I32))
        return jnp.sum(cnt, axis=1, keepdims=True)

    thr = _topk_threshold(lambda cand: count(lambda k: k >= cand), tq, n_sel)
    cnt_gt = count(lambda k: k > thr)
    cnt_ge = count(lambda k: k >= thr)
    need = n_sel - cnt_gt
    tie = (cnt_ge > n_sel) & (thr > KEY_NEG_INF)

    @pl.when(jnp.max(tie.astype(I32)) > 0)
    def _():
        r_i = lax.broadcasted_iota(I32, (LANES, LANES), 0)
        c_i = lax.broadcasted_iota(I32, (LANES, LANES), 1)
        upper = (r_i < c_i).astype(BF16)

        def body(c, seen):
            off = pl.multiple_of(c * LANES, LANES)
            k = key_scr[:, pl.ds(off, LANES)]
            eq = k == thr
            eqf = jnp.where(eq, 1.0, 0.0)
            rank = seen + _dot(eqf.astype(BF16), upper)
            demote = eq & tie & (rank >= need.astype(F32))
            key_scr[:, pl.ds(off, LANES)] = jnp.where(demote, KEY_NEG_INF, k)
            return seen + jnp.sum(eqf, axis=1, keepdims=True)

        lax.fori_loop(0, nkc * (kc // LANES), body, jnp.zeros((tq, 1), F32))

    def mask_body(c, carry):
        off = pl.multiple_of(c * kc, kc)
        k = key_scr[:, pl.ds(off, kc)]
        sel = (k >= thr) & (k > KEY_NEG_INF)
        key_scr[:, pl.ds(off, kc)] = pltpu.bitcast(jnp.where(sel, 0.0, NEG_INF), I32)
        return carry

    lax.fori_loop(0, nkc, mask_body, 0)

    pair = lambda ref, h, off: ref[0, pl.ds(off, kc), (h // 2) * LANES:(h // 2 + 1) * LANES]
    outs = []
    for h0 in range(0, N_HEADS, HEAD_GROUP):
        heads = list(range(h0, h0 + HEAD_GROUP))
        outs += _two_pass_attention(
            nkc, kc, heads, [_pair_masked(qa, h) for h in heads],
            functools.partial(pair, ka_ref), functools.partial(pair, va_ref),
            lambda h, off: pltpu.bitcast(key_scr[:, pl.ds(off, kc)], F32),
            s_scr)
    _store_pairs(o_ref, outs)


def _dsa_prompt(qi, small, qa, kid, kab, vab):
    b, t, _ = qa.shape
    tq = min(DSA_TQ, t)
    n_sel = min(TOPK_MAX, t // 4)
    rowq = lambda w: pl.BlockSpec((1, tq, w), lambda bi, i: (bi, i, 0))
    full = lambda w: pl.BlockSpec((1, t, w), lambda bi, i: (bi, 0, 0))
    return pl.pallas_call(
        functools.partial(_dsa_prompt_kernel, n_sel=n_sel),
        grid=(b, t // tq),
        in_specs=[rowq(HD), rowq(LANES), rowq(HD), full(LANES), full(HD), full(HD)],
        out_specs=rowq(HD),
        out_shape=jax.ShapeDtypeStruct((b, t, HD), BF16),
        scratch_shapes=[pltpu.VMEM((tq, t), I32), pltpu.VMEM((HEAD_GROUP, tq, t), F32)],
        compiler_params=_cparams(("arbitrary", "arbitrary")),
        name="dsa_prompt",
    )(qi, small, qa, kid, kab, vab)


FOX_TQ = 256
FOX_KC = 512


def _fox_prompt_kernel(qb_ref, small_ref, drow_ref, kb_ref, vb_ref, o_ref, s_scr):
    qt = pl.program_id(1)
    tq = qb_ref.shape[1]
    kc = FOX_KC
    nkc = ((qt + 1) * tq + kc - 1) // kc
    small = small_ref[0]
    qb = qb_ref[0]
    qpos = qt * tq + lax.broadcasted_iota(I32, (tq, kc), 0)
    lane_k = lax.broadcasted_iota(I32, (tq, kc), 1)
    pair = lambda ref, h, off: ref[0, pl.ds(off, kc), (h // 2) * LANES:(h // 2 + 1) * LANES]

    def bias(h, off):
        b = small[:, L_D + h:L_D + h + 1] - drow_ref[0, h:h + 1, pl.ds(off, kc)]
        return jnp.where(off + lane_k <= qpos, b, NEG_INF)

    outs = []
    for h0 in range(0, N_HEADS, HEAD_GROUP):
        heads = list(range(h0, h0 + HEAD_GROUP))
        outs += _two_pass_attention(
            nkc, kc, heads, [_pair_masked(qb, h) for h in heads],
            functools.partial(pair, kb_ref), functools.partial(pair, vb_ref), bias, s_scr)
    _store_pairs(o_ref, outs)


def _fox_prompt(qb, small, drow, kbb, vbb):
    b, t, _ = qb.shape
    tq = min(FOX_TQ, t)
    rowq = lambda w: pl.BlockSpec((1, tq, w), lambda bi, i: (bi, i, 0))
    full = lambda w: pl.BlockSpec((1, t, w), lambda bi, i: (bi, 0, 0))
    return pl.pallas_call(
        _fox_prompt_kernel,
        grid=(b, t // tq),
        in_specs=[rowq(HD), rowq(LANES), pl.BlockSpec((1, N_HEADS, t), lambda bi, i: (bi, 0, 0)),
                  full(HD), full(HD)],
        out_specs=rowq(HD),
        out_shape=jax.ShapeDtypeStruct((b, t, HD), BF16),
        scratch_shapes=[pltpu.VMEM((HEAD_GROUP, tq, t), F32)],
        compiler_params=_cparams(("arbitrary", "arbitrary")),
        name="fox_prompt",
    )(qb, small, drow, kbb, vbb)


def _mix_kernel(x_ref, oa_ref, ob_ref, ga_ref, gb_ref, gt1_ref, sh2_ref, sc2_ref, gn2_ref,
                woa_ref, wob_ref, wout_ref, wr_ref, br_ref, x1_o, h2_o, comb_o):
    ya = _dot(oa_ref[0], woa_ref[...])
    yb = _dot(ob_ref[0], wob_ref[...])
    mix = ga_ref[0].astype(F32) * ya + gb_ref[0].astype(F32) * yb
    x1 = x_ref[0] + gt1_ref[0] * _dot(mix.astype(BF16), wout_ref[...])
    x1_o[0] = x1
    ms = jnp.mean(x1 * x1, axis=-1, keepdims=True)
    h2 = (x1 * lax.rsqrt(ms + EPS) * gn2_ref[...]) * (1.0 + sc2_ref[0]) + sh2_ref[0]
    h2b = h2.astype(BF16)
    h2_o[0] = h2b
    logits = _dot(h2b, wr_ref[...]) + br_ref[...]
    lane = lax.broadcasted_iota(I32, logits.shape, 1)
    work = logits
    vals, idxs = [], []
    for _ in range(TOP_K_EXPERTS):
        mx = jnp.max(work, axis=1, keepdims=True)
        idx = jnp.min(jnp.where(work == mx, lane, LANES), axis=1, keepdims=True)
        vals.append(mx)
        idxs.append(idx)
        work = jnp.where(lane == idx, NEG_INF, work)
    es = [jnp.exp(v - vals[0]) for v in vals]
    denom = es[0] + es[1] + es[2] + es[3]
    comb = jnp.zeros(logits.shape, F32)
    for e, idx in zip(es, idxs):
        comb = comb + jnp.where(lane == idx, e / denom, 0.0)
    comb_o[0] = comb


def _mix(x, oa, ob, ga, gb, gt1, sh2, sc2, g_norm2, mw, tm):
    woa, wob, wout, wr, br = mw
    b, t, d = x.shape
    per_row = gt1.shape[1] != 1
    mod_spec = (pl.BlockSpec((1, tm, d), lambda bi, i: (bi, i, 0)) if per_row
                else pl.BlockSpec((1, 1, d), lambda bi, i: (bi, 0, 0)))
    const = lambda shape: pl.BlockSpec(shape, lambda bi, i: tuple(0 for _ in shape))
    row = lambda width: pl.BlockSpec((1, tm, width), lambda bi, i: (bi, i, 0))
    return pl.pallas_call(
        _mix_kernel,
        grid=(b, t // tm),
        in_specs=[row(d), row(HD), row(HD), row(d), row(d), mod_spec, mod_spec, mod_spec,
                  const((1, d)), const(woa.shape), const(wob.shape), const(wout.shape),
                  const(wr.shape), const(br.shape)],
        out_specs=[row(d), row(d), row(LANES)],
        out_shape=[jax.ShapeDtypeStruct((b, t, d), F32), jax.ShapeDtypeStruct((b, t, d), BF16),
                   jax.ShapeDtypeStruct((b, t, LANES), F32)],
        compiler_params=_cparams(("arbitrary", "arbitrary")),
        name="mix_router",
    )(x, oa, ob, ga, gb, gt1, sh2, sc2, g_norm2.reshape(1, d), woa, wob, wout, wr, br)


MOE_TM = 1024
MOE_CH = 128
MOE_RB = 256


def _moe_kernel(h_ref, comb_ref, x1_ref, gt2_ref, wup_ref, bup_ref, wdn_ref, bdn_ref, y_o,
                rank_t_ref, comb_t_ref):
    e = pl.program_id(2)
    tm = h_ref.shape[1]
    ch = MOE_CH
    rb = min(MOE_RB, tm)

    @pl.when(e == 0)
    def _():
        y_o[0] = jnp.zeros(y_o.shape[1:], F32)
        comb = comb_ref[0]
        sel = jnp.where(comb > 0.0, 1.0, 0.0).astype(BF16)
        eye = jnp.where(lax.broadcasted_iota(I32, (LANES, LANES), 0)
                        == lax.broadcasted_iota(I32, (LANES, LANES), 1), 1.0, 0.0).astype(BF16)
        sel_t = _dot_nt(eye, sel).astype(BF16)
        hi, mid, lo = _split3(comb)
        comb_t_ref[...] = _dot_nt(eye, hi) + _dot_nt(eye, mid) + _dot_nt(eye, lo)
        for blk in range(tm // rb):
            r_j = lax.broadcasted_iota(I32, (tm, rb), 0)
            c_j = blk * rb + lax.broadcasted_iota(I32, (tm, rb), 1)
            earlier_t = jnp.where(r_j < c_j, 1.0, 0.0).astype(BF16)
            rank_t_ref[:, blk * rb:(blk + 1) * rb] = _dot(sel_t, earlier_t)

    rank_row = rank_t_ref[pl.ds(e, 1), :]
    c_row = comb_t_ref[pl.ds(e, 1), :]
    n_tok = jnp.sum(jnp.where(c_row > 0.0, 1.0, 0.0)).astype(I32)

    def chunk(c, carry):
        base = (c * ch).astype(F32)
        slot_r = base + lax.broadcasted_iota(I32, (ch, tm), 0).astype(F32)
        pf = jnp.where((rank_row == slot_r) & (c_row > 0.0), 1.0, 0.0)
        c_sorted = jnp.sum(pf * c_row, axis=1, keepdims=True)
        xg = _dot(pf.astype(BF16), h_ref[0]).astype(BF16)
        hu = _dot(xg, wup_ref[0]) + bup_ref[0]
        g = jnp.minimum(hu[:, :D_FF], SWIGLU_LIMIT)
        lin = jnp.clip(hu[:, D_FF:], -SWIGLU_LIMIT, SWIGLU_LIMIT)
        act = (lin + 1.0) * g * _sigmoid(SWIGLU_ALPHA * g)
        ye = _dot(act.astype(BF16), wdn_ref[0]) + bdn_ref[0]
        yw = (c_sorted * ye).astype(BF16)
        y_o[0] += _dot(pf.T.astype(BF16), yw)
        return carry

    lax.fori_loop(0, (n_tok + ch - 1) // ch, chunk, 0)

    @pl.when(e == pl.num_programs(2) - 1)
    def _():
        y_o[0] = x1_ref[0] + gt2_ref[0] * y_o[0]


def _moe(h2, comb, x1, gt2, ew, tm):
    wup, bup, wdn, bdn = ew
    b, t, d = x1.shape
    ne = wup.shape[0]
    per_row = gt2.shape[1] != 1
    mod_spec = (pl.BlockSpec((1, tm, d), lambda bi, i, e: (bi, i, 0)) if per_row
                else pl.BlockSpec((1, 1, d), lambda bi, i, e: (bi, 0, 0)))
    row = lambda width: pl.BlockSpec((1, tm, width), lambda bi, i, e: (bi, i, 0))
    return pl.pallas_call(
        _moe_kernel,
        grid=(b, t // tm, ne),
        in_specs=[row(d), row(LANES), row(d), mod_spec,
                  pl.BlockSpec((1, d, 2 * D_FF), lambda bi, i, e: (e, 0, 0)),
                  pl.BlockSpec((1, 1, 2 * D_FF), lambda bi, i, e: (e, 0, 0)),
                  pl.BlockSpec((1, D_FF, d), lambda bi, i, e: (e, 0, 0)),
                  pl.BlockSpec((1, 1, d), lambda bi, i, e: (e, 0, 0))],
        out_specs=row(d),
        out_shape=jax.ShapeDtypeStruct((b, t, d), F32),
        scratch_shapes=[pltpu.VMEM((LANES, tm), F32), pltpu.VMEM((LANES, tm), F32)],
        compiler_params=_cparams(("arbitrary", "arbitrary", "arbitrary")),
        name="moe",
    )(h2, comb, x1, gt2, wup, bup, wdn, bdn)


def _mod_split(mod, lo, hi):
    return jnp.split(mod[lo:hi], N_ADA, axis=-1)


def _prompt_group(x, mods, pw, mw, ew, g_norm1, g_norm2):
    b, t, d = x.shape
    sh1, sc1, gt1, sh2, sc2, gt2 = [m[:, None, :] for m in mods]
    tables = _rope_tables(jnp.arange(t, dtype=I32))
    (ka, va, kb, vb, small, qa, qi, qb, kab, vab, kbb, vbb, kid, ga, gb) = _proj(
        x, g_norm1, sh1, sc1, pw, tables, tm=min(256, t))
    out_a = _dsa_prompt(qi, small, qa, kid, kab, vab)
    drow = jnp.swapaxes(small[:, :, L_D:L_D + N_HEADS], 1, 2)
    out_b = _fox_prompt(qb, small, drow, kbb, vbb)
    x1, h2, comb = _mix(x, out_a, out_b, ga, gb, gt1, sh2, sc2, g_norm2, mw, tm=min(256, t))
    y = _moe(h2, comb, x1, gt2, ew, tm=min(MOE_TM, t))
    heads = lambda z: z.reshape(1, b, t, N_HEADS, HEAD_DIM)
    return (y, heads(ka), heads(va), small[None, :, :, :D_IDX], heads(kb), heads(vb),
            small[None, :, :, L_LF:L_D])


def _mix_weights(w_oa, w_ob, w_out, w_router, b_router):
    ne = w_router.shape[1]
    wr = jnp.concatenate([w_router, jnp.zeros((w_router.shape[0], LANES - ne), w_router.dtype)], axis=1)
    br = jnp.concatenate([b_router, jnp.full((LANES - ne,), NEG_INF, b_router.dtype)]).reshape(1, LANES)
    return w_oa.astype(BF16), w_ob.astype(BF16), w_out.astype(BF16), wr.astype(BF16), br


def _moe_weights(w_up, b_up, w_down, b_down):
    ne = w_up.shape[0]
    return (w_up.astype(BF16), b_up.reshape(ne, 1, -1), w_down.astype(BF16), b_down.reshape(ne, 1, -1))


SCORE_CHUNK = 1024


def _page_gather(pt_ref, b, cache_ref, dst_slab, sem):
    npages = pt_ref.shape[1]

    def copy(p, page):
        return pltpu.make_async_copy(cache_ref.at[page], dst_slab(p), sem)

    def issue(p, c):
        copy(p, pt_ref[b, p]).start()
        return c

    def wait(p, c):
        copy(p, 0).wait()
        return c

    lax.fori_loop(0, npages, issue, 0)
    lax.fori_loop(0, npages, wait, 0)


def _sidx_kernel(pt_ref, q8_ref, w8_ref, knew_ref, cache_ref, sc_ref, kbuf, sem):
    b = pl.program_id(0)
    _page_gather(pt_ref, b, cache_ref,
                 lambda p: kbuf.at[:, pl.ds(pl.multiple_of(p * PAGE_SIZE, PAGE_SIZE), PAGE_SIZE)], sem)
    n_past = kbuf.shape[1]
    q8 = q8_ref[0]
    w8 = w8_ref[0]
    for c in range(n_past // SCORE_CHUNK):
        kb = kbuf[:, c * SCORE_CHUNK:(c + 1) * SCORE_CHUNK].astype(BF16)
        d = _dot(q8, kb)
        sc_ref[0, :, c * SCORE_CHUNK:(c + 1) * SCORE_CHUNK] = jnp.sum(
            w8 * jnp.maximum(d, 0.0), axis=0, keepdims=True)
    dn = jnp.sum(q8.astype(F32) * knew_ref[0], axis=1, keepdims=True)
    sn = jnp.sum(w8 * jnp.maximum(dn, 0.0), axis=0, keepdims=True)
    lane = lax.broadcasted_iota(I32, (1, LANES), 1)
    sc_ref[0, :, n_past:n_past + LANES] = jnp.where(lane == 0, sn, NEG_INF)


def _sample_scores(page_table, q8, w8, knew, cache_kidx):
    nb, npages = page_table.shape
    n_past = npages * PAGE_SIZE
    grid_spec = pltpu.PrefetchScalarGridSpec(
        num_scalar_prefetch=1,
        grid=(nb,),
        in_specs=[pl.BlockSpec((1, N_HEADS, D_IDX), lambda b, pt: (b, 0, 0)),
                  pl.BlockSpec((1, N_HEADS, 1), lambda b, pt: (b, 0, 0)),
                  pl.BlockSpec((1, 1, D_IDX), lambda b, pt: (b, 0, 0)),
                  pl.BlockSpec(memory_space=pl.ANY)],
        out_specs=pl.BlockSpec((1, 1, n_past + LANES), lambda b, pt: (b, 0, 0)),
        scratch_shapes=[pltpu.VMEM((D_IDX, n_past), F32), pltpu.SemaphoreType.DMA(())],
    )
    return pl.pallas_call(
        _sidx_kernel,
        grid_spec=grid_spec,
        out_shape=jax.ShapeDtypeStruct((nb, 1, n_past + LANES), F32),
        compiler_params=_cparams(("arbitrary",)),
        name="sample_scores",
    )(page_table, q8, w8, knew, cache_kidx)


def _ssel_kernel(sc_ref, o_ref, key_scr, *, n_sel):
    rows, n = sc_ref.shape
    nch = n // LANES
    key_scr[...] = _sort_key(sc_ref[...] + 0.0)

    def count(pred):
        def body(c, cnt):
            off = pl.multiple_of(c * LANES, LANES)
            return cnt + pred(key_scr[:, pl.ds(off, LANES)]).astype(I32)
        cnt = lax.fori_loop(0, nch, body, jnp.zeros((rows, LANES), I32))
        return jnp.sum(cnt, axis=1, keepdims=True)

    thr = _topk_threshold(lambda cand: count(lambda k: k >= cand), rows, n_sel)
    cnt_gt = count(lambda k: k > thr)
    cnt_ge = count(lambda k: k >= thr)
    need = n_sel - cnt_gt
    tie = (cnt_ge > n_sel) & (thr > KEY_NEG_INF)

    @pl.when(jnp.max(tie.astype(I32)) > 0)
    def _():
        r_i = lax.broadcasted_iota(I32, (LANES, LANES), 0)
        c_i = lax.broadcasted_iota(I32, (LANES, LANES), 1)
        upper = (r_i < c_i).astype(BF16)

        def body(c, seen):
            off = pl.multiple_of(c * LANES, LANES)
            k = key_scr[:, pl.ds(off, LANES)]
            eq = k == thr
            eqf = jnp.where(eq, 1.0, 0.0)
            rank = seen + _dot(eqf.astype(BF16), upper)
            demote = eq & tie & (rank >= need.astype(F32))
            key_scr[:, pl.ds(off, LANES)] = jnp.where(demote, KEY_NEG_INF, k)
            return seen + jnp.sum(eqf, axis=1, keepdims=True)

        lax.fori_loop(0, nch, body, jnp.zeros((rows, 1), F32))

    k = key_scr[...]
    o_ref[...] = jnp.where((k >= thr) & (k > KEY_NEG_INF), 0.0, NEG_INF)


def _sample_select(scores, n_keys):
    rows, n = scores.shape
    n_sel = min(TOPK_MAX, n_keys // 4)
    return pl.pallas_call(
        functools.partial(_ssel_kernel, n_sel=n_sel),
        out_shape=jax.ShapeDtypeStruct((rows, n), F32),
        scratch_shapes=[pltpu.VMEM((rows, n), I32)],
        compiler_params=pltpu.CompilerParams(vmem_limit_bytes=VMEM_LIMIT),
        name="sample_select",
    )(scores)


def _split3(x):
    hi = x.astype(BF16)
    r1 = x - hi.astype(F32)
    mid = r1.astype(BF16)
    lo = (r1 - mid.astype(F32)).astype(BF16)
    return hi, mid, lo


def _dot3_l(x, m):
    hi, mid, lo = _split3(x)
    return _dot(hi, m) + _dot(mid, m) + _dot(lo, m)


def _dot3_r(m, x):
    hi, mid, lo = _split3(x)
    return _dot(m, hi) + _dot(m, mid) + _dot(m, lo)


def _fbias_kernel(pt_ref, lfnew_ref, cache_ref, o_ref, lbuf, sem):
    b = pl.program_id(0)
    _page_gather(pt_ref, b, cache_ref, lambda p: lbuf.at[p], sem)
    npages = lbuf.shape[0]
    r_i = lax.broadcasted_iota(I32, (npages, npages), 0)
    c_i = lax.broadcasted_iota(I32, (npages, npages), 1)
    later_pages = jnp.where(c_i > r_i, 1.0, 0.0).astype(BF16)
    t_r = lax.broadcasted_iota(I32, (PAGE_SIZE, PAGE_SIZE), 0)
    t_c = lax.broadcasted_iota(I32, (PAGE_SIZE, PAGE_SIZE), 1)
    later_toks = jnp.where(t_r > t_c, 1.0, 0.0).astype(BF16)
    lfnew = lfnew_ref[0]
    for h in range(N_HEADS):
        x = lbuf[:, h, :]
        within = _dot3_l(x, later_toks)
        tot = jnp.sum(x, axis=1, keepdims=True)
        later = _dot3_r(later_pages, jnp.broadcast_to(tot, (npages, PAGE_SIZE)))
        o_ref[0, h] = within + later + lfnew[:, h:h + 1]


def _sample_fox_bias(page_table, lfnew, cache_logf_t):
    nb, npages = page_table.shape
    grid_spec = pltpu.PrefetchScalarGridSpec(
        num_scalar_prefetch=1,
        grid=(nb,),
        in_specs=[pl.BlockSpec((1, 1, N_HEADS), lambda b, pt: (b, 0, 0)),
                  pl.BlockSpec(memory_space=pl.ANY)],
        out_specs=pl.BlockSpec((1, N_HEADS, npages, PAGE_SIZE), lambda b, pt: (b, 0, 0, 0)),
        scratch_shapes=[pltpu.VMEM((npages, N_HEADS, PAGE_SIZE), F32), pltpu.SemaphoreType.DMA(())],
    )
    return pl.pallas_call(
        _fbias_kernel,
        grid_spec=grid_spec,
        out_shape=jax.ShapeDtypeStruct((nb, N_HEADS, npages, PAGE_SIZE), F32),
        compiler_params=_cparams(("arbitrary",)),
        name="sample_fox_bias",
    )(page_table, lfnew, cache_logf_t)


DEC_PAGES = 16
QROWS = 16


def _decode_kernel(pt_ref, q_ref, knew_ref, vnew_ref, bnew_ref, *refs, npg):
    k_refs, v_refs, b_refs = refs[:npg], refs[npg:2 * npg], refs[2 * npg:3 * npg]
    o_ref, m_scr, l_scr, acc_scr = refs[3 * npg:]
    s = pl.program_id(1)
    hrow = lax.broadcasted_iota(I32, (QROWS, HD), 0)
    lane = lax.broadcasted_iota(I32, (QROWS, HD), 1)
    own = (lane // HEAD_DIM) == hrow
    qf = jnp.where(own, jnp.broadcast_to(q_ref[0].astype(F32), (QROWS, HD)), 0.0)
    qm = qf.astype(BF16)

    @pl.when(s == 0)
    def _():
        sn = jnp.sum(qf[:N_HEADS] * knew_ref[0], axis=1, keepdims=True) + bnew_ref[0]
        live = sn > NEG_INF
        m_scr[...] = jnp.broadcast_to(sn, m_scr.shape)
        l_scr[...] = jnp.broadcast_to(jnp.where(live, 1.0, 0.0), l_scr.shape)
        acc_scr[...] = jnp.where(live, 1.0, 0.0) * jnp.broadcast_to(vnew_ref[0], acc_scr.shape)

    for j in range(npg):
        kp = k_refs[j][0].astype(BF16)
        vp = v_refs[j][0].astype(BF16)
        sc = _dot(qm, kp)[:N_HEADS] + b_refs[j][0, 0]
        m_old = m_scr[...]
        m_new = jnp.maximum(m_old, jnp.max(sc, axis=1, keepdims=True))
        m_safe = jnp.where(m_new == NEG_INF, 0.0, m_new)
        p = jnp.exp(sc - m_safe[:, 0:1])
        alpha = jnp.exp(m_old - m_safe)
        l_scr[...] = alpha * l_scr[...] + jnp.sum(p, axis=1, keepdims=True)
        p16 = jnp.concatenate([p, jnp.zeros_like(p)], axis=0).astype(BF16)
        acc_scr[...] = alpha[:, 0:1] * acc_scr[...] + _dot_nt(p16, vp)[:N_HEADS]
        m_scr[...] = m_new

    @pl.when(s == pl.num_programs(1) - 1)
    def _():
        o = jnp.where(own[:N_HEADS], acc_scr[...] / l_scr[:, 0:1], 0.0)
        o_ref[0] = jnp.sum(o, axis=0, keepdims=True)


def _paged_decode(page_table, q, knew, vnew, bnew, cache_k, cache_v, bias):
    nb, npages = page_table.shape
    hb = bias.shape[2]
    npg = min(DEC_PAGES, npages)
    kv_spec = lambda j: pl.BlockSpec((1, HD, PAGE_SIZE), lambda b, s, pt: (pt[b, s * npg + j], 0, 0))
    b_spec = lambda j: pl.BlockSpec((1, 1, hb, PAGE_SIZE), lambda b, s, pt: (b, s * npg + j, 0, 0))
    row = lambda w: pl.BlockSpec((1, 1, w), lambda b, s, pt: (b, 0, 0))
    grid_spec = pltpu.PrefetchScalarGridSpec(
        num_scalar_prefetch=1,
        grid=(nb, npages // npg),
        in_specs=[row(HD), row(HD), row(HD), pl.BlockSpec((1, N_HEADS, 1), lambda b, s, pt: (b, 0, 0))]
                 + [kv_spec(j) for j in range(npg)] * 2 + [b_spec(j) for j in range(npg)],
        out_specs=row(HD),
        scratch_shapes=[pltpu.VMEM((N_HEADS, LANES), F32), pltpu.VMEM((N_HEADS, LANES), F32),
                        pltpu.VMEM((N_HEADS, HD), F32)],
    )
    return pl.pallas_call(
        functools.partial(_decode_kernel, npg=npg),
        grid_spec=grid_spec,
        out_shape=jax.ShapeDtypeStruct((nb, 1, HD), F32),
        compiler_params=_cparams(("arbitrary", "arbitrary")),
        name="paged_decode",
    )(page_table, q, knew, vnew, bnew, *([cache_k] * npg), *([cache_v] * npg), *([bias] * npg))


def _sample_group(x, mods, pw, mw, ew, g_norm1, g_norm2, caches, page_table, pos):
    nb, t, d = x.shape
    cache_k_a, cache_v_a, cache_kidx, cache_k_b, cache_v_b, cache_logf = caches
    n_pool = cache_k_a.shape[0]
    npages = page_table.shape[1]
    n_past = npages * PAGE_SIZE
    sh1, sc1, gt1, sh2, sc2, gt2 = [m[None] for m in mods]
    xs = x.reshape(1, nb, d)
    tables = _rope_tables(jnp.full((nb,), pos, I32))
    (ka, va, kb, vb, small, qa, qi, qb, _, _, _, _, _, ga, gb) = _proj(
        xs, g_norm1, sh1, sc1, pw, tables, tm=nb)
    col = lambda z: z.reshape(nb, 1, -1)
    kv_t = lambda c: jnp.transpose(c, (0, 2, 3, 1)).reshape(n_pool, HD, PAGE_SIZE)
    scores = _sample_scores(page_table, qi.reshape(nb, N_HEADS, D_IDX),
                            small[0, :, L_WI:L_LF].reshape(nb, N_HEADS, 1),
                            small[0, :, :D_IDX].reshape(nb, 1, D_IDX), jnp.swapaxes(cache_kidx, 1, 2))
    mask = _sample_select(scores.reshape(nb, -1), n_past + t)
    bias_a = mask[:, :n_past].reshape(nb, npages, 1, PAGE_SIZE)
    bnew_a = jnp.broadcast_to(mask[:, n_past:n_past + 1, None], (nb, N_HEADS, 1))
    out_a = _paged_decode(page_table, col(qa), col(ka), col(va), bnew_a,
                          kv_t(cache_k_a), kv_t(cache_v_a), bias_a)
    logf = small[0, :, L_LF:L_D]
    bias_b = _sample_fox_bias(page_table, logf.reshape(nb, 1, N_HEADS), jnp.swapaxes(cache_logf, 1, 2))
    out_b = _paged_decode(page_table, col(qb), col(kb), col(vb), jnp.zeros((nb, N_HEADS, 1), F32),
                          kv_t(cache_k_b), kv_t(cache_v_b), jnp.swapaxes(bias_b, 1, 2))
    oa = out_a.reshape(1, nb, HD).astype(BF16)
    ob = out_b.reshape(1, nb, HD).astype(BF16)
    x1, h2, comb = _mix(xs, oa, ob, ga, gb, gt1, sh2, sc2, g_norm2, mw, tm=nb)
    rows = -(-nb // MOE_CH) * MOE_CH
    padr = lambda z: jnp.pad(z, ((0, 0), (0, rows - nb), (0, 0)))
    y = _moe(padr(h2), padr(comb), padr(x1), padr(gt2), ew, tm=rows)[:, :nb]
    heads = lambda z: z.reshape(1, nb, t, N_HEADS, HEAD_DIM)
    return (y.reshape(nb, t, d), heads(ka), heads(va), small[0, :, :D_IDX].reshape(1, nb, t, D_IDX),
            heads(kb), heads(vb), logf.reshape(1, nb, t, N_HEADS))


def kernel(x_prompt, x_sample, c_prompt, c_sample, cache_k_a, cache_v_a, cache_kidx_a, cache_k_b,
           cache_v_b, cache_logf_b, page_table, w_ada, b_ada, g_norm1, w_in, b_f, g_qa, g_ka, g_kidx,
           g_qb, g_kb, w_oa, w_ob, w_out, g_norm2, w_router, b_router, w_up, b_up, w_down, b_down):
    depth = w_in.shape[0]
    assert depth == 1, "single-layer trunk"
    l = 0
    nbp, nbs = x_prompt.shape[0], x_sample.shape[0]
    n_past = page_table.shape[1] * PAGE_SIZE
    rows = nbp + nbs
    pad = (-rows) % 8
    c_all = jnp.concatenate([c_prompt, c_sample, jnp.zeros((pad, c_prompt.shape[1]), F32)], axis=0)
    mod = _adaln(c_all, w_ada[l], b_ada[l])
    pw = _proj_weights(w_in[l], b_f[l], g_qa[l], g_ka[l], g_kidx[l], g_qb[l], g_kb[l])
    mw = _mix_weights(w_oa[l], w_ob[l], w_out[l], w_router[l], b_router[l])
    ew = _moe_weights(w_up[l], b_up[l], w_down[l], b_down[l])
    outs_p = _prompt_group(x_prompt, _mod_split(mod, 0, nbp), pw, mw, ew, g_norm1[l], g_norm2[l])
    caches = (cache_k_a[l], cache_v_a[l], cache_kidx_a[l], cache_k_b[l], cache_v_b[l], cache_logf_b[l])
    outs_s = _sample_group(x_sample, _mod_split(mod, nbp, rows), pw, mw, ew, g_norm1[l], g_norm2[l],
                           caches, page_table, n_past)
    return (outs_p[0], outs_s[0]) + tuple(outs_p[1:]) + tuple(outs_s[1:])
```

```python
import functools

import numpy as np
import jax
import jax.numpy as jnp
from jax import lax
from jax.experimental import pallas as pl
from jax.experimental.pallas import tpu as pltpu

F32 = jnp.float32
BF16 = jnp.bfloat16
I32 = jnp.int32

D_MODEL = 1024
HEAD_DIM = 64
N_HEADS = 8
HD = N_HEADS * HEAD_DIM
D_IDX = 64
ROT_DIM = HEAD_DIM // 4
ROPE_THETA = 500000.0
TOPK_MAX = 256
N_EXPERTS = 32
TOP_K_EXPERTS = 4
D_FF = D_MODEL
SWIGLU_LIMIT = 7.0
SWIGLU_ALPHA = 1.702
EPS = 1e-6
N_ADA = 6
PAGE_SIZE = 128
LANES = 128
W_IDX_SCALE = N_HEADS ** -0.5 * D_IDX ** -0.5
QK_SCALE = HEAD_DIM ** -0.5

C_QA, C_KA, C_VA, C_QI, C_QB, C_KB, C_VB = (i * HD for i in range(7))
C_GA = 7 * HD
C_GB = C_GA + D_MODEL
C_SM = C_GB + D_MODEL
D_IN_PAD = C_SM + LANES
L_WI = D_IDX
L_LF = D_IDX + N_HEADS
L_D = L_LF + N_HEADS

NEG_INF = float("-inf")
INT_MIN = -2 ** 31
KEY_NEG_INF = int(np.int32(np.uint32(0xFF800000) ^ np.uint32(0x7FFFFFFF)))

VMEM_LIMIT = 56 * 1024 * 1024


def _cparams(sem):
    return pltpu.CompilerParams(dimension_semantics=sem, vmem_limit_bytes=VMEM_LIMIT)


def _sigmoid(x):
    return 1.0 / (1.0 + jnp.exp(-x))


def _dot(a, b):
    return jnp.dot(a, b, preferred_element_type=F32)


def _dot_nt(a, b):
    return lax.dot_general(a, b, (((1,), (1,)), ((), ())), preferred_element_type=F32)


def _adaln_kernel(c_ref, w_ref, b_ref, o_ref):
    c = c_ref[...]
    s = c * _sigmoid(c)
    o_ref[...] = _dot(s.astype(BF16), w_ref[...].astype(BF16)) + b_ref[...]


def _adaln(c, w_ada, b_ada):
    rows, d = c.shape
    n = w_ada.shape[1]
    tn = 1536
    return pl.pallas_call(
        _adaln_kernel,
        grid=(n // tn,),
        in_specs=[pl.BlockSpec((rows, d), lambda j: (0, 0)),
                  pl.BlockSpec((d, tn), lambda j: (0, j)),
                  pl.BlockSpec((1, tn), lambda j: (0, j))],
        out_specs=pl.BlockSpec((rows, tn), lambda j: (0, j)),
        out_shape=jax.ShapeDtypeStruct((rows, n), F32),
        compiler_params=_cparams(("arbitrary",)),
        name="adaln",
    )(c, w_ada, b_ada.reshape(1, n))


def _rope(z, cos, sp, sm):
    w = z.shape[1]
    return z * cos + pltpu.roll(z, 8, 1) * sp + pltpu.roll(z, w - 8, 1) * sm


def _tile4(t):
    return jnp.concatenate([t, t, t, t], axis=1)


def _proj_kernel(x_ref, gn_ref, sh_ref, sc_ref, w_ref, gains_ref, gsm_ref, bf_ref,
                 cos_ref, sp_ref, sm_ref, gmat_ref, gsmat_ref,
                 ka_o, va_o, kb_o, vb_o, small_o,
                 qa_o, qi_o, qb_o, kab_o, vab_o, kbb_o, vbb_o, kid_o, ga_o, gb_o,
                 carry_ref):
    i = pl.program_id(1)
    tm = x_ref.shape[1]

    @pl.when(i == 0)
    def _():
        carry_ref[...] = jnp.zeros_like(carry_ref)

    x = x_ref[0]
    ms = jnp.mean(x * x, axis=-1, keepdims=True)
    y = x * lax.rsqrt(ms + EPS) * gn_ref[...]
    h = y * (1.0 + sc_ref[0]) + sh_ref[0]
    hb = h.astype(BF16)

    cos128, sp128, sm128 = cos_ref[...], sp_ref[...], sm_ref[...]
    cos, sp, sm = _tile4(cos128), _tile4(sp128), _tile4(sm128)
    gmat = gmat_ref[...]

    def sect(c0, width=HD):
        return _dot(hb, w_ref[:, c0:c0 + width])

    def headnorm(z, g):
        msq = _dot((z * z).astype(BF16), gmat)
        return z * lax.rsqrt(msq + EPS) * g

    qa = _rope(headnorm(sect(C_QA), gains_ref[0:1, :]), cos, sp, sm)
    qa_o[0] = (qa * QK_SCALE).astype(BF16)
    ka = _rope(headnorm(sect(C_KA), gains_ref[1:2, :]), cos, sp, sm)
    ka_o[0] = ka
    kab_o[0] = ka.astype(BF16)
    va = sect(C_VA)
    va_o[0] = va
    vab_o[0] = va.astype(BF16)
    qi_o[0] = _rope(sect(C_QI), cos, sp, sm).astype(qi_o.dtype)
    qb_o[0] = (headnorm(sect(C_QB), gains_ref[2:3, :]) * QK_SCALE).astype(BF16)
    kb = headnorm(sect(C_KB), gains_ref[3:4, :])
    kb_o[0] = kb
    kbb_o[0] = kb.astype(BF16)
    vb = sect(C_VB)
    vb_o[0] = vb
    vbb_o[0] = vb.astype(BF16)
    ga_o[0] = _sigmoid(sect(C_GA, D_MODEL)).astype(BF16)
    gb_o[0] = _sigmoid(sect(C_GB, D_MODEL)).astype(BF16)

    zs = sect(C_SM, LANES)
    lane = lax.broadcasted_iota(I32, (tm, LANES), 1)
    msq = _dot((zs * zs).astype(BF16), gsmat_ref[...])
    ki = _rope(zs * lax.rsqrt(msq + EPS) * gsm_ref[...], cos128, sp128, sm128)
    kid_o[0] = (ki + pltpu.roll(ki, D_IDX, 1)).astype(BF16)
    t = zs + bf_ref[...]
    logf = jnp.minimum(t, 0.0) - jnp.log1p(jnp.exp(-jnp.abs(t)))
    lf_only = jnp.where((lane >= L_LF) & (lane < L_D), logf, 0.0)
    r_i = lax.broadcasted_iota(I32, (tm, tm), 0)
    c_i = lax.broadcasted_iota(I32, (tm, tm), 1)
    tri = (c_i <= r_i).astype(F32)
    dcum = jnp.dot(tri, lf_only, preferred_element_type=F32,
                   precision=lax.Precision.HIGHEST) + carry_ref[...]
    carry_ref[...] = dcum[tm - 1:tm, :]
    small = (ki + jnp.where((lane >= L_WI) & (lane < L_LF), zs * W_IDX_SCALE, 0.0)
             + lf_only + pltpu.roll(dcum, N_HEADS, 1))
    small_o[0] = small


def _rope_tables(pos):
    half = ROT_DIM // 2
    inv_freq = ROPE_THETA ** (-jnp.arange(half, dtype=F32) * 2.0 / ROT_DIM)
    ang = pos.astype(F32)[:, None] * inv_freq[None, :]
    cos8, sin8 = jnp.cos(ang), jnp.sin(ang)
    t = pos.shape[0]
    one = jnp.ones((t, HEAD_DIM - ROT_DIM), F32)
    zero = jnp.zeros((t, HEAD_DIM - ROT_DIM), F32)
    z8 = jnp.zeros((t, half), F32)
    cos64 = jnp.concatenate([cos8, cos8, one], axis=1)
    sp64 = jnp.concatenate([z8, sin8, zero], axis=1)
    sm64 = jnp.concatenate([-sin8, z8, zero], axis=1)
    dup = lambda a: jnp.concatenate([a, a], axis=1)
    return dup(cos64), dup(sp64), dup(sm64)


def _proj_weights(w_in, b_f, g_qa, g_ka, g_kidx, g_qb, g_kb):
    offs = np.cumsum((HD, HD, HD, HD, D_IDX, N_HEADS, HD, HD, HD, N_HEADS, D_MODEL, D_MODEL))[:-1].tolist()
    qa, ka, va, qi, ki, wi, qb, kb, vb, fb, ga, gb = jnp.split(w_in, offs, axis=1)
    pad = jnp.zeros((w_in.shape[0], LANES - D_IDX - 2 * N_HEADS), w_in.dtype)
    w = jnp.concatenate([qa, ka, va, qi, qb, kb, vb, ga, gb, ki, wi, fb, pad], axis=1).astype(BF16)
    tile8 = lambda g: jnp.tile(g, N_HEADS)
    gains = jnp.stack([tile8(g_qa), tile8(g_ka), tile8(g_qb), tile8(g_kb)]
                      + [jnp.zeros((HD,), F32)] * 4)
    gsm = jnp.concatenate([g_kidx, jnp.zeros((LANES - D_IDX,), F32)]).reshape(1, LANES)
    bfv = jnp.zeros((LANES,), F32).at[L_LF:L_D].set(b_f).reshape(1, LANES)
    blk = np.kron(np.eye(N_HEADS), np.full((HEAD_DIM, HEAD_DIM), 1.0 / HEAD_DIM))
    gmat = jnp.asarray(blk, BF16)
    gs = np.zeros((LANES, LANES))
    gs[:D_IDX, :D_IDX] = 1.0 / D_IDX
    return w, gains, gsm, bfv, gmat, jnp.asarray(gs, BF16)


def _proj(x, g_norm, shift, scale, pw, tables, tm, qi_dtype=BF16):
    w, gains, gsm, bfv, gmat, gsmat = pw
    b, t, d = x.shape
    per_row = shift.shape[1] != 1
    mod_spec = (pl.BlockSpec((1, tm, d), lambda bi, i: (bi, i, 0)) if per_row
                else pl.BlockSpec((1, 1, d), lambda bi, i: (bi, 0, 0)))
    const = lambda shape: pl.BlockSpec(shape, lambda bi, i: tuple(0 for _ in shape))
    tab_spec = pl.BlockSpec((tm, LANES), lambda bi, i: (i, 0))
    row = lambda width: pl.BlockSpec((1, tm, width), lambda bi, i: (bi, i, 0))
    f = lambda width, dt: jax.ShapeDtypeStruct((b, t, width), dt)
    return pl.pallas_call(
        _proj_kernel,
        grid=(b, t // tm),
        in_specs=[row(d), const((1, d)), mod_spec, mod_spec, const(w.shape), const(gains.shape),
                  const((1, LANES)), const((1, LANES)), tab_spec, tab_spec, tab_spec,
                  const(gmat.shape), const(gsmat.shape)],
        out_specs=[row(HD)] * 4 + [row(LANES)] + [row(HD)] * 7 + [row(LANES)] + [row(d)] * 2,
        out_shape=[f(HD, F32)] * 4 + [f(LANES, F32)] + [f(HD, BF16), f(HD, qi_dtype)]
                  + [f(HD, BF16)] * 5 + [f(LANES, BF16)] + [f(d, BF16)] * 2,
        scratch_shapes=[pltpu.VMEM((1, LANES), F32)],
        compiler_params=_cparams(("arbitrary", "arbitrary")),
        name="proj",
    )(x, g_norm.reshape(1, d), shift, scale, w, gains, gsm, bfv, *tables, gmat, gsmat)


DSA_TQ = 256
DSA_KC = 512
DSA_SC = 256
HEAD_GROUP = 4


def _pair_masked(q, h):
    pair = q[:, (h // 2) * LANES:(h // 2 + 1) * LANES].astype(F32)
    lane = lax.broadcasted_iota(I32, pair.shape, 1)
    keep = (lane < HEAD_DIM) if h % 2 == 0 else (lane >= HEAD_DIM)
    return jnp.where(keep, pair, 0.0).astype(BF16)


def _key_to_f32(key):
    return pltpu.bitcast(key ^ ((key >> 31) & jnp.int32(0x7FFFFFFF)), F32)


def _topk_threshold(count_ge, rows, k):
    zero = jnp.zeros((rows, 1), I32)
    t0 = jnp.where(count_ge(_key_to_f32(zero)) >= k, zero, jnp.full((rows, 1), INT_MIN, I32))

    def body(j, t):
        cand = t | (jnp.int32(1) << (30 - j))
        return jnp.where(count_ge(_key_to_f32(cand)) >= k, cand, t)

    t = lax.fori_loop(0, 31, body, t0)
    has_k = t > KEY_NEG_INF
    return jnp.where(has_k, _key_to_f32(t), NEG_INF), has_k


def _topk_mask_inplace(sc_scr, nch, cw, n_sel):
    rows = sc_scr.shape[0]

    def count(pred):
        def body(c, cnt):
            off = pl.multiple_of(c * cw, cw)
            return cnt + _fold_lanes(jnp.where(pred(sc_scr[:, pl.ds(off, cw)]), 1, 0))
        cnt = lax.fori_loop(0, nch, body, jnp.zeros((rows, LANES), I32))
        return jnp.sum(cnt, axis=1, keepdims=True)

    thr, has_k = _topk_threshold(lambda cand: count(lambda s: s >= cand), rows, n_sel)
    need = n_sel - count(lambda s: s > thr)
    tie = (count(lambda s: s >= thr) > n_sel) & has_k

    @pl.when(jnp.max(tie.astype(I32)) > 0)
    def _():
        r_i = lax.broadcasted_iota(I32, (LANES, LANES), 0)
        c_i = lax.broadcasted_iota(I32, (LANES, LANES), 1)
        upper = jnp.where(r_i < c_i, 1.0, 0.0).astype(BF16)

        def body(c, seen):
            off = pl.multiple_of(c * LANES, LANES)
            s = sc_scr[:, pl.ds(off, LANES)]
            eq = s == thr
            eqf = jnp.where(eq, 1.0, 0.0)
            rank = seen + _dot(eqf.astype(BF16), upper)
            demote = eq & tie & (rank >= need.astype(F32))
            sc_scr[:, pl.ds(off, LANES)] = jnp.where(demote, NEG_INF, s)
            return seen + jnp.sum(eqf, axis=1, keepdims=True)

        lax.fori_loop(0, nch * (cw // LANES), body, jnp.zeros((rows, 1), F32))

    def mask_body(c, carry):
        off = pl.multiple_of(c * cw, cw)
        s = sc_scr[:, pl.ds(off, cw)]
        sc_scr[:, pl.ds(off, cw)] = jnp.where((s >= thr) & (s > NEG_INF), 0.0, NEG_INF)
        return carry

    lax.fori_loop(0, nch, mask_body, 0)


def _fold_lanes(m):
    out = m[:, 0:LANES]
    for j in range(1, m.shape[1] // LANES):
        out = out + m[:, j * LANES:(j + 1) * LANES]
    return out


def _two_pass_attention(nkc, kc, heads, qm, k_chunk, v_chunk, bias_chunk, s_scr):
    tq = qm[0].shape[0]
    ng = len(heads)

    def logits_body(c, mxs):
        off = pl.multiple_of(c * kc, kc)
        out = []
        for g, h in enumerate(heads):
            s = _dot_nt(qm[g], k_chunk(h, off)) + bias_chunk(h, off)
            s_scr[g, :, pl.ds(off, kc)] = s
            mx = mxs[g]
            for j in range(kc // LANES):
                mx = jnp.maximum(mx, s[:, j * LANES:(j + 1) * LANES])
            out.append(mx)
        return tuple(out)

    neg = jnp.full((tq, LANES), NEG_INF, F32)
    mxs = lax.fori_loop(0, nkc, logits_body, (neg,) * ng)
    ms = []
    for mx in mxs:
        m = jnp.max(mx, axis=1, keepdims=True)
        ms.append(jnp.where(m == NEG_INF, 0.0, m))

    def pv_body(c, carry):
        off = pl.multiple_of(c * kc, kc)
        out = []
        for g, h in enumerate(heads):
            l, acc = carry[g]
            p = jnp.exp(s_scr[g, :, pl.ds(off, kc)] - ms[g])
            out.append((l + _fold_lanes(p), acc + _dot(p.astype(BF16), v_chunk(h, off))))
        return tuple(out)

    zero = jnp.zeros((tq, LANES), F32)
    res = lax.fori_loop(0, nkc, pv_body, ((zero, zero),) * ng)
    return [acc / jnp.sum(l, axis=1, keepdims=True) for l, acc in res]


def _store_pairs(o_ref, outs):
    lane = lax.broadcasted_iota(I32, outs[0].shape, 1)
    for p in range(N_HEADS // 2):
        o_ref[0, :, p * LANES:(p + 1) * LANES] = jnp.where(
            lane < HEAD_DIM, outs[2 * p], outs[2 * p + 1]).astype(o_ref.dtype)


def _dsa_prompt_kernel(qi_ref, small_ref, qa_ref, kid_ref, ka_ref, va_ref, o_ref,
                       sc_scr, s_scr, *, n_sel):
    qt = pl.program_id(1)
    tq = qi_ref.shape[1]
    kc, sc = DSA_KC, DSA_SC
    nkc = ((qt + 1) * tq + kc - 1) // kc
    small = small_ref[0]
    qi = qi_ref[0]
    qa = qa_ref[0]
    qpos = qt * tq + lax.broadcasted_iota(I32, (tq, sc), 0)
    lane_k = lax.broadcasted_iota(I32, (tq, sc), 1)
    qim = [_pair_masked(qi, h) for h in range(N_HEADS)]

    def score_body(c, carry):
        off = pl.multiple_of(c * sc, sc)
        kid = kid_ref[0, pl.ds(off, sc), :]
        acc = jnp.zeros((tq, sc), F32)
        for h in range(N_HEADS):
            d = _dot_nt(qim[h], kid)
            acc = acc + small[:, L_WI + h:L_WI + h + 1] * jnp.maximum(d, 0.0)
        sc_scr[:, pl.ds(off, sc)] = jnp.where(off + lane_k <= qpos, acc, NEG_INF)
        return carry

    lax.fori_loop(0, nkc * (kc // sc), score_body, 0)

    _topk_mask_inplace(sc_scr, nkc, kc, n_sel)

    pair = lambda ref, h, off: ref[0, pl.ds(off, kc), (h // 2) * LANES:(h // 2 + 1) * LANES]
    outs = []
    for h0 in range(0, N_HEADS, HEAD_GROUP):
        heads = list(range(h0, h0 + HEAD_GROUP))
        outs += _two_pass_attention(
            nkc, kc, heads, [_pair_masked(qa, h) for h in heads],
            functools.partial(pair, ka_ref), functools.partial(pair, va_ref),
            lambda h, off: sc_scr[:, pl.ds(off, kc)], s_scr)
    _store_pairs(o_ref, outs)


def _dsa_prompt(qi, small, qa, kid, kab, vab):
    b, t, _ = qa.shape
    tq = min(DSA_TQ, t)
    n_sel = min(TOPK_MAX, t // 4)
    rowq = lambda w: pl.BlockSpec((1, tq, w), lambda bi, i: (bi, i, 0))
    full = lambda w: pl.BlockSpec((1, t, w), lambda bi, i: (bi, 0, 0))
    return pl.pallas_call(
        functools.partial(_dsa_prompt_kernel, n_sel=n_sel),
        grid=(b, t // tq),
        in_specs=[rowq(HD), rowq(LANES), rowq(HD), full(LANES), full(HD), full(HD)],
        out_specs=rowq(HD),
        out_shape=jax.ShapeDtypeStruct((b, t, HD), BF16),
        scratch_shapes=[pltpu.VMEM((tq, t), F32), pltpu.VMEM((HEAD_GROUP, tq, t), F32)],
        compiler_params=_cparams(("arbitrary", "arbitrary")),
        name="dsa_prompt",
    )(qi, small, qa, kid, kab, vab)


FOX_TQ = 256
FOX_KC = 512


def _fox_prompt_kernel(qb_ref, small_ref, drow_ref, kb_ref, vb_ref, o_ref, s_scr):
    qt = pl.program_id(1)
    tq = qb_ref.shape[1]
    kc = FOX_KC
    nkc = ((qt + 1) * tq + kc - 1) // kc
    small = small_ref[0]
    qb = qb_ref[0]
    qpos = qt * tq + lax.broadcasted_iota(I32, (tq, kc), 0)
    lane_k = lax.broadcasted_iota(I32, (tq, kc), 1)
    pair = lambda ref, h, off: ref[0, pl.ds(off, kc), (h // 2) * LANES:(h // 2 + 1) * LANES]

    def bias(h, off):
        b = small[:, L_D + h:L_D + h + 1] - drow_ref[0, h:h + 1, pl.ds(off, kc)]
        return jnp.where(off + lane_k <= qpos, b, NEG_INF)

    outs = []
    for h0 in range(0, N_HEADS, HEAD_GROUP):
        heads = list(range(h0, h0 + HEAD_GROUP))
        outs += _two_pass_attention(
            nkc, kc, heads, [_pair_masked(qb, h) for h in heads],
            functools.partial(pair, kb_ref), functools.partial(pair, vb_ref), bias, s_scr)
    _store_pairs(o_ref, outs)


def _fox_prompt(qb, small, drow, kbb, vbb):
    b, t, _ = qb.shape
    tq = min(FOX_TQ, t)
    rowq = lambda w: pl.BlockSpec((1, tq, w), lambda bi, i: (bi, i, 0))
    full = lambda w: pl.BlockSpec((1, t, w), lambda bi, i: (bi, 0, 0))
    return pl.pallas_call(
        _fox_prompt_kernel,
        grid=(b, t // tq),
        in_specs=[rowq(HD), rowq(LANES), pl.BlockSpec((1, N_HEADS, t), lambda bi, i: (bi, 0, 0)),
                  full(HD), full(HD)],
        out_specs=rowq(HD),
        out_shape=jax.ShapeDtypeStruct((b, t, HD), BF16),
        scratch_shapes=[pltpu.VMEM((HEAD_GROUP, tq, t), F32)],
        compiler_params=_cparams(("arbitrary", "arbitrary")),
        name="fox_prompt",
    )(qb, small, drow, kbb, vbb)


def _mix_kernel(x_ref, oa_ref, ob_ref, ga_ref, gb_ref, gt1_ref, sh2_ref, sc2_ref, gn2_ref,
                woa_ref, wob_ref, wout_ref, wr_ref, br_ref, x1_o, h2_o, comb_o):
    ya = _dot(oa_ref[0], woa_ref[...])
    yb = _dot(ob_ref[0], wob_ref[...])
    mix = ga_ref[0].astype(F32) * ya + gb_ref[0].astype(F32) * yb
    x1 = x_ref[0] + gt1_ref[0] * _dot(mix.astype(BF16), wout_ref[...])
    x1_o[0] = x1
    ms = jnp.mean(x1 * x1, axis=-1, keepdims=True)
    h2 = (x1 * lax.rsqrt(ms + EPS) * gn2_ref[...]) * (1.0 + sc2_ref[0]) + sh2_ref[0]
    h2b = h2.astype(BF16)
    h2_o[0] = h2b
    wr = wr_ref[...]
    w_hi = wr.astype(BF16)
    w_lo = (wr - w_hi.astype(F32)).astype(BF16)
    h_lo = (h2 - h2b.astype(F32)).astype(BF16)
    logits = _dot(h2b, w_hi) + _dot(h2b, w_lo) + _dot(h_lo, w_hi) + br_ref[...]
    lane = lax.broadcasted_iota(I32, logits.shape, 1)
    work = logits
    vals, idxs = [], []
    for _ in range(TOP_K_EXPERTS):
        mx = jnp.max(work, axis=1, keepdims=True)
        idx = jnp.min(jnp.where(work == mx, lane, LANES), axis=1, keepdims=True)
        vals.append(mx)
        idxs.append(idx)
        work = jnp.where(lane == idx, NEG_INF, work)
    es = [jnp.exp(v - vals[0]) for v in vals]
    denom = es[0] + es[1] + es[2] + es[3]
    comb = jnp.zeros(logits.shape, F32)
    for e, idx in zip(es, idxs):
        comb = comb + jnp.where(lane == idx, e / denom, 0.0)
    comb_o[0] = comb


def _mix(x, oa, ob, ga, gb, gt1, sh2, sc2, g_norm2, mw, tm):
    woa, wob, wout, wr, br = mw
    b, t, d = x.shape
    per_row = gt1.shape[1] != 1
    mod_spec = (pl.BlockSpec((1, tm, d), lambda bi, i: (bi, i, 0)) if per_row
                else pl.BlockSpec((1, 1, d), lambda bi, i: (bi, 0, 0)))
    const = lambda shape: pl.BlockSpec(shape, lambda bi, i: tuple(0 for _ in shape))
    row = lambda width: pl.BlockSpec((1, tm, width), lambda bi, i: (bi, i, 0))
    return pl.pallas_call(
        _mix_kernel,
        grid=(b, t // tm),
        in_specs=[row(d), row(HD), row(HD), row(d), row(d), mod_spec, mod_spec, mod_spec,
                  const((1, d)), const(woa.shape), const(wob.shape), const(wout.shape),
                  const(wr.shape), const(br.shape)],
        out_specs=[row(d), row(d), row(LANES)],
        out_shape=[jax.ShapeDtypeStruct((b, t, d), F32), jax.ShapeDtypeStruct((b, t, d), BF16),
                   jax.ShapeDtypeStruct((b, t, LANES), F32)],
        compiler_params=_cparams(("arbitrary", "arbitrary")),
        name="mix_router",
    )(x, oa, ob, ga, gb, gt1, sh2, sc2, g_norm2.reshape(1, d), woa, wob, wout, wr, br)


MOE_TM = 1024
MOE_CH = 128
MOE_RB = 256


def _moe_kernel(h_ref, comb_ref, x1_ref, gt2_ref, wup_ref, bup_ref, wdn_ref, bdn_ref, y_o,
                rank_t_ref, comb_t_ref):
    e = pl.program_id(2)
    tm = h_ref.shape[1]
    ch = MOE_CH
    rb = min(MOE_RB, tm)

    @pl.when(e == 0)
    def _():
        y_o[0] = jnp.zeros(y_o.shape[1:], F32)
        comb = comb_ref[0]
        sel = jnp.where(comb > 0.0, 1.0, 0.0).astype(BF16)
        eye = jnp.where(lax.broadcasted_iota(I32, (LANES, LANES), 0)
                        == lax.broadcasted_iota(I32, (LANES, LANES), 1), 1.0, 0.0).astype(BF16)
        sel_t = _dot_nt(eye, sel).astype(BF16)
        hi, mid, lo = _split3(comb)
        comb_t_ref[...] = _dot_nt(eye, hi) + _dot_nt(eye, mid) + _dot_nt(eye, lo)
        for blk in range(tm // rb):
            r_j = lax.broadcasted_iota(I32, (tm, rb), 0)
            c_j = blk * rb + lax.broadcasted_iota(I32, (tm, rb), 1)
            earlier_t = jnp.where(r_j < c_j, 1.0, 0.0).astype(BF16)
            rank_t_ref[:, blk * rb:(blk + 1) * rb] = _dot(sel_t, earlier_t)

    rank_row = rank_t_ref[pl.ds(e, 1), :]
    c_row = comb_t_ref[pl.ds(e, 1), :]
    n_tok = jnp.sum(jnp.where(c_row > 0.0, 1.0, 0.0)).astype(I32)

    def chunk(c, carry):
        base = (c * ch).astype(F32)
        slot_r = base + lax.broadcasted_iota(I32, (ch, tm), 0).astype(F32)
        pf = jnp.where((rank_row == slot_r) & (c_row > 0.0), 1.0, 0.0)
        c_sorted = jnp.sum(pf * c_row, axis=1, keepdims=True)
        xg = _dot(pf.astype(BF16), h_ref[0]).astype(BF16)
        hu = _dot(xg, wup_ref[0]) + bup_ref[0]
        g = jnp.minimum(hu[:, :D_FF], SWIGLU_LIMIT)
        lin = jnp.clip(hu[:, D_FF:], -SWIGLU_LIMIT, SWIGLU_LIMIT)
        act = (lin + 1.0) * g * _sigmoid(SWIGLU_ALPHA * g)
        ye = _dot(act.astype(BF16), wdn_ref[0]) + bdn_ref[0]
        yw = (c_sorted * ye).astype(BF16)
        y_o[0] += _dot(pf.T.astype(BF16), yw)
        return carry

    lax.fori_loop(0, (n_tok + ch - 1) // ch, chunk, 0)

    @pl.when(e == pl.num_programs(2) - 1)
    def _():
        y_o[0] = x1_ref[0] + gt2_ref[0] * y_o[0]


def _moe(h2, comb, x1, gt2, ew, tm):
    wup, bup, wdn, bdn = ew
    b, t, d = x1.shape
    ne = wup.shape[0]
    per_row = gt2.shape[1] != 1
    mod_spec = (pl.BlockSpec((1, tm, d), lambda bi, i, e: (bi, i, 0)) if per_row
                else pl.BlockSpec((1, 1, d), lambda bi, i, e: (bi, 0, 0)))
    row = lambda width: pl.BlockSpec((1, tm, width), lambda bi, i, e: (bi, i, 0))
    return pl.pallas_call(
        _moe_kernel,
        grid=(b, t // tm, ne),
        in_specs=[row(d), row(LANES), row(d), mod_spec,
                  pl.BlockSpec((1, d, 2 * D_FF), lambda bi, i, e: (e, 0, 0)),
                  pl.BlockSpec((1, 1, 2 * D_FF), lambda bi, i, e: (e, 0, 0)),
                  pl.BlockSpec((1, D_FF, d), lambda bi, i, e: (e, 0, 0)),
                  pl.BlockSpec((1, 1, d), lambda bi, i, e: (e, 0, 0))],
        out_specs=row(d),
        out_shape=jax.ShapeDtypeStruct((b, t, d), F32),
        scratch_shapes=[pltpu.VMEM((LANES, tm), F32), pltpu.VMEM((LANES, tm), F32)],
        compiler_params=_cparams(("arbitrary", "arbitrary", "arbitrary")),
        name="moe",
    )(h2, comb, x1, gt2, wup, bup, wdn, bdn)


def _mod_split(mod, lo, hi):
    return jnp.split(mod[lo:hi], N_ADA, axis=-1)


def _prompt_group(x, mods, pw, mw, ew, g_norm1, g_norm2):
    b, t, d = x.shape
    sh1, sc1, gt1, sh2, sc2, gt2 = [m[:, None, :] for m in mods]
    tables = _rope_tables(jnp.arange(t, dtype=I32))
    (ka, va, kb, vb, small, qa, qi, qb, kab, vab, kbb, vbb, kid, ga, gb) = _proj(
        x, g_norm1, sh1, sc1, pw, tables, tm=min(256, t))
    out_a = _dsa_prompt(qi, small, qa, kid, kab, vab)
    drow = jnp.swapaxes(small[:, :, L_D:L_D + N_HEADS], 1, 2)
    out_b = _fox_prompt(qb, small, drow, kbb, vbb)
    x1, h2, comb = _mix(x, out_a, out_b, ga, gb, gt1, sh2, sc2, g_norm2, mw, tm=min(256, t))
    y = _moe(h2, comb, x1, gt2, ew, tm=min(MOE_TM, t))
    heads = lambda z: z.reshape(1, b, t, N_HEADS, HEAD_DIM)
    return (y, heads(ka), heads(va), small[None, :, :, :D_IDX], heads(kb), heads(vb),
            small[None, :, :, L_LF:L_D])


def _mix_weights(w_oa, w_ob, w_out, w_router, b_router):
    ne = w_router.shape[1]
    wr = jnp.concatenate([w_router, jnp.zeros((w_router.shape[0], LANES - ne), w_router.dtype)], axis=1)
    br = jnp.concatenate([b_router, jnp.full((LANES - ne,), NEG_INF, b_router.dtype)]).reshape(1, LANES)
    return w_oa.astype(BF16), w_ob.astype(BF16), w_out.astype(BF16), wr, br


def _moe_weights(w_up, b_up, w_down, b_down):
    ne = w_up.shape[0]
    return (w_up.astype(BF16), b_up.reshape(ne, 1, -1), w_down.astype(BF16), b_down.reshape(ne, 1, -1))


SCORE_CHUNK = 1024


def _page_gather(pt_ref, b, cache_ref, dst_slab, sem):
    npages = pt_ref.shape[1]

    def copy(p, page):
        return pltpu.make_async_copy(cache_ref.at[page], dst_slab(p), sem)

    def issue(p, c):
        copy(p, pt_ref[b, p]).start()
        return c

    def wait(p, c):
        copy(p, 0).wait()
        return c

    lax.fori_loop(0, npages, issue, 0)
    lax.fori_loop(0, npages, wait, 0)


def _sidx_kernel(pt_ref, q8_ref, w8_ref, knew_ref, cache_ref, sc_ref, kbuf, sem):
    b = pl.program_id(0)
    _page_gather(pt_ref, b, cache_ref,
                 lambda p: kbuf.at[:, pl.ds(pl.multiple_of(p * PAGE_SIZE, PAGE_SIZE), PAGE_SIZE)], sem)
    n_past = kbuf.shape[1]
    q8 = q8_ref[0]
    w8 = w8_ref[0]
    q_hi = q8.astype(BF16)
    q_lo = (q8 - q_hi.astype(F32)).astype(BF16)
    for c in range(n_past // SCORE_CHUNK):
        kf = kbuf[:, c * SCORE_CHUNK:(c + 1) * SCORE_CHUNK]
        k_hi = kf.astype(BF16)
        k_lo = (kf - k_hi.astype(F32)).astype(BF16)
        d = _dot(q_hi, k_hi) + _dot(q_hi, k_lo) + _dot(q_lo, k_hi)
        sc_ref[0, :, c * SCORE_CHUNK:(c + 1) * SCORE_CHUNK] = jnp.sum(
            w8 * jnp.maximum(d, 0.0), axis=0, keepdims=True)
    dn = jnp.sum(q8 * knew_ref[0], axis=1, keepdims=True)
    sn = jnp.sum(w8 * jnp.maximum(dn, 0.0), axis=0, keepdims=True)
    lane = lax.broadcasted_iota(I32, (1, LANES), 1)
    sc_ref[0, :, n_past:n_past + LANES] = jnp.where(lane == 0, sn, NEG_INF)


def _sample_scores(page_table, q8, w8, knew, cache_kidx):
    nb, npages = page_table.shape
    n_past = npages * PAGE_SIZE
    grid_spec = pltpu.PrefetchScalarGridSpec(
        num_scalar_prefetch=1,
        grid=(nb,),
        in_specs=[pl.BlockSpec((1, N_HEADS, D_IDX), lambda b, pt: (b, 0, 0)),
                  pl.BlockSpec((1, N_HEADS, 1), lambda b, pt: (b, 0, 0)),
                  pl.BlockSpec((1, 1, D_IDX), lambda b, pt: (b, 0, 0)),
                  pl.BlockSpec(memory_space=pl.ANY)],
        out_specs=pl.BlockSpec((1, 1, n_past + LANES), lambda b, pt: (b, 0, 0)),
        scratch_shapes=[pltpu.VMEM((D_IDX, n_past), F32), pltpu.SemaphoreType.DMA(())],
    )
    return pl.pallas_call(
        _sidx_kernel,
        grid_spec=grid_spec,
        out_shape=jax.ShapeDtypeStruct((nb, 1, n_past + LANES), F32),
        compiler_params=_cparams(("arbitrary",)),
        name="sample_scores",
    )(page_table, q8, w8, knew, cache_kidx)


def _ssel_kernel(sc_ref, o_ref, *, n_sel):
    o_ref[...] = sc_ref[...]
    _topk_mask_inplace(o_ref, o_ref.shape[1] // LANES, LANES, n_sel)


def _sample_select(scores, n_keys):
    rows, n = scores.shape
    n_sel = min(TOPK_MAX, n_keys // 4)
    return pl.pallas_call(
        functools.partial(_ssel_kernel, n_sel=n_sel),
        out_shape=jax.ShapeDtypeStruct((rows, n), F32),
        compiler_params=pltpu.CompilerParams(vmem_limit_bytes=VMEM_LIMIT),
        name="sample_select",
    )(scores)


def _split3(x):
    hi = x.astype(BF16)
    r1 = x - hi.astype(F32)
    mid = r1.astype(BF16)
    lo = (r1 - mid.astype(F32)).astype(BF16)
    return hi, mid, lo


def _dot3_l(x, m):
    hi, mid, lo = _split3(x)
    return _dot(hi, m) + _dot(mid, m) + _dot(lo, m)


def _dot3_r(m, x):
    hi, mid, lo = _split3(x)
    return _dot(m, hi) + _dot(m, mid) + _dot(m, lo)


def _fbias_kernel(pt_ref, lfnew_ref, cache_ref, o_ref, lbuf, sem):
    b = pl.program_id(0)
    _page_gather(pt_ref, b, cache_ref, lambda p: lbuf.at[p], sem)
    npages = lbuf.shape[0]
    r_i = lax.broadcasted_iota(I32, (npages, npages), 0)
    c_i = lax.broadcasted_iota(I32, (npages, npages), 1)
    later_pages = jnp.where(c_i > r_i, 1.0, 0.0).astype(BF16)
    t_r = lax.broadcasted_iota(I32, (PAGE_SIZE, PAGE_SIZE), 0)
    t_c = lax.broadcasted_iota(I32, (PAGE_SIZE, PAGE_SIZE), 1)
    later_toks = jnp.where(t_r > t_c, 1.0, 0.0).astype(BF16)
    lfnew = lfnew_ref[0]
    for h in range(N_HEADS):
        x = lbuf[:, h, :]
        within = _dot3_l(x, later_toks)
        tot = jnp.sum(x, axis=1, keepdims=True)
        later = _dot3_r(later_pages, jnp.broadcast_to(tot, (npages, PAGE_SIZE)))
        o_ref[0, h] = within + later + lfnew[:, h:h + 1]


def _sample_fox_bias(page_table, lfnew, cache_logf_t):
    nb, npages = page_table.shape
    grid_spec = pltpu.PrefetchScalarGridSpec(
        num_scalar_prefetch=1,
        grid=(nb,),
        in_specs=[pl.BlockSpec((1, 1, N_HEADS), lambda b, pt: (b, 0, 0)),
                  pl.BlockSpec(memory_space=pl.ANY)],
        out_specs=pl.BlockSpec((1, N_HEADS, npages, PAGE_SIZE), lambda b, pt: (b, 0, 0, 0)),
        scratch_shapes=[pltpu.VMEM((npages, N_HEADS, PAGE_SIZE), F32), pltpu.SemaphoreType.DMA(())],
    )
    return pl.pallas_call(
        _fbias_kernel,
        grid_spec=grid_spec,
        out_shape=jax.ShapeDtypeStruct((nb, N_HEADS, npages, PAGE_SIZE), F32),
        compiler_params=_cparams(("arbitrary",)),
        name="sample_fox_bias",
    )(page_table, lfnew, cache_logf_t)


DEC_PAGES = 16
QROWS = 16


def _decode_kernel(pt_ref, q_ref, knew_ref, vnew_ref, bnew_ref, *refs, npg):
    k_refs, v_refs, b_refs = refs[:npg], refs[npg:2 * npg], refs[2 * npg:3 * npg]
    o_ref, m_scr, l_scr, acc_scr = refs[3 * npg:]
    s = pl.program_id(1)
    hrow = lax.broadcasted_iota(I32, (QROWS, HD), 0)
    lane = lax.broadcasted_iota(I32, (QROWS, HD), 1)
    own = (lane // HEAD_DIM) == hrow
    qf = jnp.where(own, jnp.broadcast_to(q_ref[0].astype(F32), (QROWS, HD)), 0.0)
    qm = qf.astype(BF16)

    @pl.when(s == 0)
    def _():
        sn = jnp.sum(qf[:N_HEADS] * knew_ref[0], axis=1, keepdims=True) + bnew_ref[0]
        live = sn > NEG_INF
        m_scr[...] = jnp.broadcast_to(sn, m_scr.shape)
        l_scr[...] = jnp.broadcast_to(jnp.where(live, 1.0, 0.0), l_scr.shape)
        acc_scr[...] = jnp.where(live, 1.0, 0.0) * jnp.broadcast_to(vnew_ref[0], acc_scr.shape)

    sc = jnp.concatenate(
        [_dot(qm, k_refs[j][0].astype(BF16))[:N_HEADS] + b_refs[j][0, 0] for j in range(npg)], axis=1)
    m_old = m_scr[...]
    m_new = jnp.maximum(m_old, jnp.max(sc, axis=1, keepdims=True))
    m_safe = jnp.where(m_new == NEG_INF, 0.0, m_new)
    p = jnp.exp(sc - m_safe[:, 0:1])
    alpha = jnp.exp(m_old - m_safe)
    l_scr[...] = alpha * l_scr[...] + jnp.sum(p, axis=1, keepdims=True)
    p16 = jnp.concatenate([p, jnp.zeros_like(p)], axis=0).astype(BF16)
    pv = _dot_nt(p16[:, 0:PAGE_SIZE], v_refs[0][0].astype(BF16))
    for j in range(1, npg):
        pv = pv + _dot_nt(p16[:, j * PAGE_SIZE:(j + 1) * PAGE_SIZE], v_refs[j][0].astype(BF16))
    acc_scr[...] = alpha[:, 0:1] * acc_scr[...] + pv[:N_HEADS]
    m_scr[...] = m_new

    @pl.when(s == pl.num_programs(1) - 1)
    def _():
        o = jnp.where(own[:N_HEADS], acc_scr[...] / l_scr[:, 0:1], 0.0)
        o_ref[0] = jnp.sum(o, axis=0, keepdims=True)


def _paged_decode(page_table, q, knew, vnew, bnew, cache_k, cache_v, bias):
    nb, npages = page_table.shape
    hb = bias.shape[2]
    npg = min(DEC_PAGES, npages)
    kv_spec = lambda j: pl.BlockSpec((1, HD, PAGE_SIZE), lambda b, s, pt: (pt[b, s * npg + j], 0, 0))
    b_spec = lambda j: pl.BlockSpec((1, 1, hb, PAGE_SIZE), lambda b, s, pt: (b, s * npg + j, 0, 0))
    row = lambda w: pl.BlockSpec((1, 1, w), lambda b, s, pt: (b, 0, 0))
    grid_spec = pltpu.PrefetchScalarGridSpec(
        num_scalar_prefetch=1,
        grid=(nb, npages // npg),
        in_specs=[row(HD), row(HD), row(HD), pl.BlockSpec((1, N_HEADS, 1), lambda b, s, pt: (b, 0, 0))]
                 + [kv_spec(j) for j in range(npg)] * 2 + [b_spec(j) for j in range(npg)],
        out_specs=row(HD),
        scratch_shapes=[pltpu.VMEM((N_HEADS, LANES), F32), pltpu.VMEM((N_HEADS, LANES), F32),
                        pltpu.VMEM((N_HEADS, HD), F32)],
    )
    return pl.pallas_call(
        functools.partial(_decode_kernel, npg=npg),
        grid_spec=grid_spec,
        out_shape=jax.ShapeDtypeStruct((nb, 1, HD), F32),
        compiler_params=_cparams(("arbitrary", "arbitrary")),
        name="paged_decode",
    )(page_table, q, knew, vnew, bnew, *([cache_k] * npg), *([cache_v] * npg), *([bias] * npg))


def _sample_group(x, mods, pw, mw, ew, g_norm1, g_norm2, caches, page_table, pos):
    nb, t, d = x.shape
    cache_k_a, cache_v_a, cache_kidx, cache_k_b, cache_v_b, cache_logf = caches
    n_pool = cache_k_a.shape[0]
    npages = page_table.shape[1]
    n_past = npages * PAGE_SIZE
    sh1, sc1, gt1, sh2, sc2, gt2 = [m[None] for m in mods]
    xs = x.reshape(1, nb, d)
    tables = _rope_tables(jnp.full((nb,), pos, I32))
    (ka, va, kb, vb, small, qa, qi, qb, _, _, _, _, _, ga, gb) = _proj(
        xs, g_norm1, sh1, sc1, pw, tables, tm=nb, qi_dtype=F32)
    col = lambda z: z.reshape(nb, 1, -1)
    kv_t = lambda c: jnp.transpose(c, (0, 2, 3, 1)).reshape(n_pool, HD, PAGE_SIZE)
    scores = _sample_scores(page_table, qi.reshape(nb, N_HEADS, D_IDX),
                            small[0, :, L_WI:L_LF].reshape(nb, N_HEADS, 1),
                            small[0, :, :D_IDX].reshape(nb, 1, D_IDX), jnp.swapaxes(cache_kidx, 1, 2))
    mask = _sample_select(scores.reshape(nb, -1), n_past + t)
    bias_a = mask[:, :n_past].reshape(nb, npages, 1, PAGE_SIZE)
    bnew_a = jnp.broadcast_to(mask[:, n_past:n_past + 1, None], (nb, N_HEADS, 1))
    out_a = _paged_decode(page_table, col(qa), col(ka), col(va), bnew_a,
                          kv_t(cache_k_a), kv_t(cache_v_a), bias_a)
    logf = small[0, :, L_LF:L_D]
    bias_b = _sample_fox_bias(page_table, logf.reshape(nb, 1, N_HEADS), jnp.swapaxes(cache_logf, 1, 2))
    out_b = _paged_decode(page_table, col(qb), col(kb), col(vb), jnp.zeros((nb, N_HEADS, 1), F32),
                          kv_t(cache_k_b), kv_t(cache_v_b), jnp.swapaxes(bias_b, 1, 2))
    oa = out_a.reshape(1, nb, HD).astype(BF16)
    ob = out_b.reshape(1, nb, HD).astype(BF16)
    x1, h2, comb = _mix(xs, oa, ob, ga, gb, gt1, sh2, sc2, g_norm2, mw, tm=nb)
    rows = -(-nb // MOE_CH) * MOE_CH
    padr = lambda z: jnp.pad(z, ((0, 0), (0, rows - nb), (0, 0)))
    y = _moe(padr(h2), padr(comb), padr(x1), padr(gt2), ew, tm=rows)[:, :nb]
    heads = lambda z: z.reshape(1, nb, t, N_HEADS, HEAD_DIM)
    return (y.reshape(nb, t, d), heads(ka), heads(va), small[0, :, :D_IDX].reshape(1, nb, t, D_IDX),
            heads(kb), heads(vb), logf.reshape(1, nb, t, N_HEADS))


def kernel(x_prompt, x_sample, c_prompt, c_sample, cache_k_a, cache_v_a, cache_kidx_a, cache_k_b,
           cache_v_b, cache_logf_b, page_table, w_ada, b_ada, g_norm1, w_in, b_f, g_qa, g_ka, g_kidx,
           g_qb, g_kb, w_oa, w_ob, w_out, g_norm2, w_router, b_router, w_up, b_up, w_down, b_down):
    depth = w_in.shape[0]
    assert depth == 1, "single-layer trunk"
    l = 0
    nbp, nbs = x_prompt.shape[0], x_sample.shape[0]
    n_past = page_table.shape[1] * PAGE_SIZE
    rows = nbp + nbs
    pad = (-rows) % 8
    c_all = jnp.concatenate([c_prompt, c_sample, jnp.zeros((pad, c_prompt.shape[1]), F32)], axis=0)
    mod = _adaln(c_all, w_ada[l], b_ada[l])
    pw = _proj_weights(w_in[l], b_f[l], g_qa[l], g_ka[l], g_kidx[l], g_qb[l], g_kb[l])
    mw = _mix_weights(w_oa[l], w_ob[l], w_out[l], w_router[l], b_router[l])
    ew = _moe_weights(w_up[l], b_up[l], w_down[l], b_down[l])
    outs_p = _prompt_group(x_prompt, _mod_split(mod, 0, nbp), pw, mw, ew, g_norm1[l], g_norm2[l])
    caches = (cache_k_a[l], cache_v_a[l], cache_kidx_a[l], cache_k_b[l], cache_v_b[l], cache_logf_b[l])
    outs_s = _sample_group(x_sample, _mod_split(mod, nbp, rows), pw, mw, ew, g_norm1[l], g_norm2[l],
                           caches, page_table, n_past)
    return (outs_p[0], outs_s[0]) + tuple(outs_p[1:]) + tuple(outs_s[1:])
```

```python
import functools

import numpy as np
import jax
import jax.numpy as jnp
from jax import lax
from jax.experimental import pallas as pl
from jax.experimental.pallas import tpu as pltpu

F32 = jnp.float32
BF16 = jnp.bfloat16
I32 = jnp.int32

D_MODEL = 1024
HEAD_DIM = 64
N_HEADS = 8
HD = N_HEADS * HEAD_DIM
D_IDX = 64
ROT_DIM = HEAD_DIM // 4
ROPE_THETA = 500000.0
TOPK_MAX = 256
N_EXPERTS = 32
TOP_K_EXPERTS = 4
D_FF = D_MODEL
SWIGLU_LIMIT = 7.0
SWIGLU_ALPHA = 1.702
EPS = 1e-6
N_ADA = 6
PAGE_SIZE = 128
LANES = 128
W_IDX_SCALE = N_HEADS ** -0.5 * D_IDX ** -0.5
LOG2E = float(np.log2(np.e))
QK_SCALE = HEAD_DIM ** -0.5 * LOG2E

C_QA, C_KA, C_VA, C_QI, C_QB, C_KB, C_VB = (i * HD for i in range(7))
C_GA = 7 * HD
C_GB = C_GA + D_MODEL
C_SM = C_GB + D_MODEL
D_IN_PAD = C_SM + LANES
L_WI = D_IDX
L_LF = D_IDX + N_HEADS
L_D = L_LF + N_HEADS

NEG_INF = float("-inf")
INT_MIN = -2 ** 31
KEY_NEG_INF = int(np.int32(np.uint32(0xFF800000) ^ np.uint32(0x7FFFFFFF)))

VMEM_LIMIT = 56 * 1024 * 1024


def _cparams(sem):
    return pltpu.CompilerParams(dimension_semantics=sem, vmem_limit_bytes=VMEM_LIMIT)


def _sigmoid(x):
    return 1.0 / (1.0 + jnp.exp(-x))


def _dot(a, b):
    return jnp.dot(a, b, preferred_element_type=F32)


def _dot_nt(a, b):
    return lax.dot_general(a, b, (((1,), (1,)), ((), ())), preferred_element_type=F32)


def _adaln_kernel(c_ref, w_ref, b_ref, o_ref):
    c = c_ref[...]
    s = c * _sigmoid(c)
    o_ref[...] = _dot(s.astype(BF16), w_ref[...].astype(BF16)) + b_ref[...]


def _adaln(c, w_ada, b_ada):
    rows, d = c.shape
    n = w_ada.shape[1]
    tn = 1536
    return pl.pallas_call(
        _adaln_kernel,
        grid=(n // tn,),
        in_specs=[pl.BlockSpec((rows, d), lambda j: (0, 0)),
                  pl.BlockSpec((d, tn), lambda j: (0, j)),
                  pl.BlockSpec((1, tn), lambda j: (0, j))],
        out_specs=pl.BlockSpec((rows, tn), lambda j: (0, j)),
        out_shape=jax.ShapeDtypeStruct((rows, n), F32),
        compiler_params=_cparams(("arbitrary",)),
        name="adaln",
    )(c, w_ada, b_ada.reshape(1, n))


def _rope(z, cos, sp, sm):
    w = z.shape[1]
    return z * cos + pltpu.roll(z, 8, 1) * sp + pltpu.roll(z, w - 8, 1) * sm


def _tile4(t):
    return jnp.concatenate([t, t, t, t], axis=1)


def _proj_kernel(x_ref, gn_ref, sh_ref, sc_ref, w_ref, gains_ref, gsm_ref, bf_ref,
                 cos_ref, sp_ref, sm_ref, gmat_ref, gsmat_ref,
                 ka_o, va_o, kb_o, vb_o, small_o,
                 qa_o, qi_o, qb_o, kab_o, vab_o, kbb_o, vbb_o, kid_o, ga_o, gb_o,
                 carry_ref):
    i = pl.program_id(1)
    tm = x_ref.shape[1]

    @pl.when(i == 0)
    def _():
        carry_ref[...] = jnp.zeros_like(carry_ref)

    x = x_ref[0]
    ms = jnp.mean(x * x, axis=-1, keepdims=True)
    y = x * lax.rsqrt(ms + EPS) * gn_ref[...]
    h = y * (1.0 + sc_ref[0]) + sh_ref[0]
    hb = h.astype(BF16)

    cos128, sp128, sm128 = cos_ref[...], sp_ref[...], sm_ref[...]
    cos, sp, sm = _tile4(cos128), _tile4(sp128), _tile4(sm128)
    gmat = gmat_ref[...]

    def sect(c0, width=HD):
        return _dot(hb, w_ref[:, c0:c0 + width])

    def headnorm(z, g):
        msq = _dot((z * z).astype(BF16), gmat)
        return z * lax.rsqrt(msq + EPS) * g

    qa = _rope(headnorm(sect(C_QA), gains_ref[0:1, :]), cos, sp, sm)
    qa_o[0] = (qa * QK_SCALE).astype(BF16)
    ka = _rope(headnorm(sect(C_KA), gains_ref[1:2, :]), cos, sp, sm)
    ka_o[0] = ka
    kab_o[0] = ka.astype(BF16)
    va = sect(C_VA)
    va_o[0] = va
    vab_o[0] = va.astype(BF16)
    qi_o[0] = _rope(sect(C_QI), cos, sp, sm).astype(qi_o.dtype)
    qb_o[0] = (headnorm(sect(C_QB), gains_ref[2:3, :]) * QK_SCALE).astype(BF16)
    kb = headnorm(sect(C_KB), gains_ref[3:4, :])
    kb_o[0] = kb
    kbb_o[0] = kb.astype(BF16)
    vb = sect(C_VB)
    vb_o[0] = vb
    vbb_o[0] = vb.astype(BF16)
    ga_o[0] = _sigmoid(sect(C_GA, D_MODEL)).astype(BF16)
    gb_o[0] = _sigmoid(sect(C_GB, D_MODEL)).astype(BF16)

    zs = sect(C_SM, LANES)
    lane = lax.broadcasted_iota(I32, (tm, LANES), 1)
    msq = _dot((zs * zs).astype(BF16), gsmat_ref[...])
    ki = _rope(zs * lax.rsqrt(msq + EPS) * gsm_ref[...], cos128, sp128, sm128)
    kid_o[0] = (ki + pltpu.roll(ki, D_IDX, 1)).astype(BF16)
    t = zs + bf_ref[...]
    logf = jnp.minimum(t, 0.0) - jnp.log1p(jnp.exp(-jnp.abs(t)))
    lf_only = jnp.where((lane >= L_LF) & (lane < L_D), logf, 0.0)
    r_i = lax.broadcasted_iota(I32, (tm, tm), 0)
    c_i = lax.broadcasted_iota(I32, (tm, tm), 1)
    tri = (c_i <= r_i).astype(F32)
    dcum = jnp.dot(tri, lf_only, preferred_element_type=F32,
                   precision=lax.Precision.HIGHEST) + carry_ref[...]
    carry_ref[...] = dcum[tm - 1:tm, :]
    small = (ki + jnp.where((lane >= L_WI) & (lane < L_LF), zs * W_IDX_SCALE, 0.0)
             + lf_only + pltpu.roll(dcum, N_HEADS, 1))
    small_o[0] = small


def _rope_tables(pos):
    half = ROT_DIM // 2
    inv_freq = ROPE_THETA ** (-jnp.arange(half, dtype=F32) * 2.0 / ROT_DIM)
    ang = pos.astype(F32)[:, None] * inv_freq[None, :]
    cos8, sin8 = jnp.cos(ang), jnp.sin(ang)
    t = pos.shape[0]
    one = jnp.ones((t, HEAD_DIM - ROT_DIM), F32)
    zero = jnp.zeros((t, HEAD_DIM - ROT_DIM), F32)
    z8 = jnp.zeros((t, half), F32)
    cos64 = jnp.concatenate([cos8, cos8, one], axis=1)
    sp64 = jnp.concatenate([z8, sin8, zero], axis=1)
    sm64 = jnp.concatenate([-sin8, z8, zero], axis=1)
    dup = lambda a: jnp.concatenate([a, a], axis=1)
    return dup(cos64), dup(sp64), dup(sm64)


def _proj_weights(w_in, b_f, g_qa, g_ka, g_kidx, g_qb, g_kb):
    offs = np.cumsum((HD, HD, HD, HD, D_IDX, N_HEADS, HD, HD, HD, N_HEADS, D_MODEL, D_MODEL))[:-1].tolist()
    qa, ka, va, qi, ki, wi, qb, kb, vb, fb, ga, gb = jnp.split(w_in, offs, axis=1)
    pad = jnp.zeros((w_in.shape[0], LANES - D_IDX - 2 * N_HEADS), w_in.dtype)
    w = jnp.concatenate([qa, ka, va, qi, qb, kb, vb, ga, gb, ki, wi, fb, pad], axis=1).astype(BF16)
    tile8 = lambda g: jnp.tile(g, N_HEADS)
    gains = jnp.stack([tile8(g_qa), tile8(g_ka), tile8(g_qb), tile8(g_kb)]
                      + [jnp.zeros((HD,), F32)] * 4)
    gsm = jnp.concatenate([g_kidx, jnp.zeros((LANES - D_IDX,), F32)]).reshape(1, LANES)
    bfv = jnp.zeros((LANES,), F32).at[L_LF:L_D].set(b_f).reshape(1, LANES)
    blk = np.kron(np.eye(N_HEADS), np.full((HEAD_DIM, HEAD_DIM), 1.0 / HEAD_DIM))
    gmat = jnp.asarray(blk, BF16)
    gs = np.zeros((LANES, LANES))
    gs[:D_IDX, :D_IDX] = 1.0 / D_IDX
    return w, gains, gsm, bfv, gmat, jnp.asarray(gs, BF16)


def _proj(x, g_norm, shift, scale, pw, tables, tm, qi_dtype=BF16):
    w, gains, gsm, bfv, gmat, gsmat = pw
    b, t, d = x.shape
    per_row = shift.shape[1] != 1
    mod_spec = (pl.BlockSpec((1, tm, d), lambda bi, i: (bi, i, 0)) if per_row
                else pl.BlockSpec((1, 1, d), lambda bi, i: (bi, 0, 0)))
    const = lambda shape: pl.BlockSpec(shape, lambda bi, i: tuple(0 for _ in shape))
    tab_spec = pl.BlockSpec((tm, LANES), lambda bi, i: (i, 0))
    row = lambda width: pl.BlockSpec((1, tm, width), lambda bi, i: (bi, i, 0))
    f = lambda width, dt: jax.ShapeDtypeStruct((b, t, width), dt)
    return pl.pallas_call(
        _proj_kernel,
        grid=(b, t // tm),
        in_specs=[row(d), const((1, d)), mod_spec, mod_spec, const(w.shape), const(gains.shape),
                  const((1, LANES)), const((1, LANES)), tab_spec, tab_spec, tab_spec,
                  const(gmat.shape), const(gsmat.shape)],
        out_specs=[row(HD)] * 4 + [row(LANES)] + [row(HD)] * 7 + [row(LANES)] + [row(d)] * 2,
        out_shape=[f(HD, F32)] * 4 + [f(LANES, F32)] + [f(HD, BF16), f(HD, qi_dtype)]
                  + [f(HD, BF16)] * 5 + [f(LANES, BF16)] + [f(d, BF16)] * 2,
        scratch_shapes=[pltpu.VMEM((1, LANES), F32)],
        compiler_params=_cparams(("arbitrary", "arbitrary")),
        name="proj",
    )(x, g_norm.reshape(1, d), shift, scale, w, gains, gsm, bfv, *tables, gmat, gsmat)


DSA_TQ = 256
DSA_KC = 512
DSA_SC = 256
HEAD_GROUP = 4


def _pair_masked(q, h):
    pair = q[:, (h // 2) * LANES:(h // 2 + 1) * LANES].astype(F32)
    lane = lax.broadcasted_iota(I32, pair.shape, 1)
    keep = (lane < HEAD_DIM) if h % 2 == 0 else (lane >= HEAD_DIM)
    return jnp.where(keep, pair, 0.0).astype(BF16)


def _key_to_f32(key):
    return pltpu.bitcast(key ^ ((key >> 31) & jnp.int32(0x7FFFFFFF)), F32)


def _topk_threshold(count_ge, rows, k):
    zero = jnp.zeros((rows, 1), I32)
    (n0,) = count_ge([_key_to_f32(zero)])
    t0 = jnp.where(n0 >= k, zero, jnp.full((rows, 1), INT_MIN, I32))

    def body(j, t):
        cand = t | (jnp.int32(1) << (30 - j))
        (n,) = count_ge([_key_to_f32(cand)])
        return jnp.where(n >= k, cand, t)

    t = lax.fori_loop(0, 31, body, t0)
    has_k = t > KEY_NEG_INF
    return jnp.where(has_k, _key_to_f32(t), NEG_INF), has_k


def _topk_mask_inplace(sc_scr, nch, cw, n_sel):
    rows = sc_scr.shape[0]

    def count(tests):
        def body(c, cnts):
            off = pl.multiple_of(c * cw, cw)
            s = sc_scr[:, pl.ds(off, cw)]
            return tuple(cnt + _fold_lanes(jnp.where(cmp(s, cand), 1, 0))
                         for cnt, (cmp, cand) in zip(cnts, tests))
        cnts = lax.fori_loop(0, nch, body, (jnp.zeros((rows, LANES), I32),) * len(tests))
        return [jnp.sum(cnt, axis=1, keepdims=True) for cnt in cnts]

    ge = lambda s, cand: s >= cand
    gt = lambda s, cand: s > cand
    thr, has_k = _topk_threshold(lambda cands: count([(ge, c) for c in cands]), rows, n_sel)
    n_gt, n_ge = count([(gt, thr), (ge, thr)])
    need = n_sel - n_gt
    tie = (n_ge > n_sel) & has_k

    @pl.when(jnp.max(tie.astype(I32)) > 0)
    def _():
        r_i = lax.broadcasted_iota(I32, (LANES, LANES), 0)
        c_i = lax.broadcasted_iota(I32, (LANES, LANES), 1)
        upper = jnp.where(r_i < c_i, 1.0, 0.0).astype(BF16)

        def body(c, seen):
            off = pl.multiple_of(c * LANES, LANES)
            s = sc_scr[:, pl.ds(off, LANES)]
            eq = s == thr
            eqf = jnp.where(eq, 1.0, 0.0)
            rank = seen + _dot(eqf.astype(BF16), upper)
            demote = eq & tie & (rank >= need.astype(F32))
            sc_scr[:, pl.ds(off, LANES)] = jnp.where(demote, NEG_INF, s)
            return seen + jnp.sum(eqf, axis=1, keepdims=True)

        lax.fori_loop(0, nch * (cw // LANES), body, jnp.zeros((rows, 1), F32))

    def mask_body(c, carry):
        off = pl.multiple_of(c * cw, cw)
        s = sc_scr[:, pl.ds(off, cw)]
        sc_scr[:, pl.ds(off, cw)] = jnp.where((s >= thr) & (s > NEG_INF), 0.0, NEG_INF)
        return carry

    lax.fori_loop(0, nch, mask_body, 0)


def _fold_lanes(m):
    out = m[:, 0:LANES]
    for j in range(1, m.shape[1] // LANES):
        out = out + m[:, j * LANES:(j + 1) * LANES]
    return out


def _two_pass_attention(nkc, kc, heads, qm, k_chunk, v_chunk, bias_chunk, s_scr):
    tq = qm[0].shape[0]
    ng = len(heads)

    def logits_body(c, mxs):
        off = pl.multiple_of(c * kc, kc)
        out = []
        for g, h in enumerate(heads):
            s = _dot_nt(qm[g], k_chunk(h, off)) + bias_chunk(h, off)
            s_scr[g, :, pl.ds(off, kc)] = s
            mx = mxs[g]
            for j in range(kc // LANES):
                mx = jnp.maximum(mx, s[:, j * LANES:(j + 1) * LANES])
            out.append(mx)
        return tuple(out)

    neg = jnp.full((tq, LANES), NEG_INF, F32)
    mxs = lax.fori_loop(0, nkc, logits_body, (neg,) * ng)
    ms = []
    for mx in mxs:
        m = jnp.max(mx, axis=1, keepdims=True)
        ms.append(jnp.where(m == NEG_INF, 0.0, m))

    def pv_body(c, carry):
        off = pl.multiple_of(c * kc, kc)
        out = []
        for g, h in enumerate(heads):
            l, acc = carry[g]
            p = jnp.exp2(s_scr[g, :, pl.ds(off, kc)] - ms[g])
            out.append((l + _fold_lanes(p), acc + _dot(p.astype(BF16), v_chunk(h, off))))
        return tuple(out)

    zero = jnp.zeros((tq, LANES), F32)
    res = lax.fori_loop(0, nkc, pv_body, ((zero, zero),) * ng)
    return [acc / jnp.sum(l, axis=1, keepdims=True) for l, acc in res]


def _store_pairs(o_ref, outs):
    lane = lax.broadcasted_iota(I32, outs[0].shape, 1)
    for p in range(N_HEADS // 2):
        o_ref[0, :, p * LANES:(p + 1) * LANES] = jnp.where(
            lane < HEAD_DIM, outs[2 * p], outs[2 * p + 1]).astype(o_ref.dtype)


def _dsa_prompt_kernel(qi_ref, small_ref, qa_ref, kid_ref, ka_ref, va_ref, o_ref,
                       sc_scr, s_scr, *, n_sel):
    qt = pl.program_id(1)
    tq = qi_ref.shape[1]
    kc, sc = DSA_KC, DSA_SC
    nkc = ((qt + 1) * tq + kc - 1) // kc
    small = small_ref[0]
    qi = qi_ref[0]
    qa = qa_ref[0]
    qpos = qt * tq + lax.broadcasted_iota(I32, (tq, sc), 0)
    lane_k = lax.broadcasted_iota(I32, (tq, sc), 1)
    qim = [_pair_masked(qi, h) for h in range(N_HEADS)]

    def score_body(c, carry):
        off = pl.multiple_of(c * sc, sc)
        kid = kid_ref[0, pl.ds(off, sc), :]
        acc = jnp.zeros((tq, sc), F32)
        for h in range(N_HEADS):
            d = _dot_nt(qim[h], kid)
            acc = acc + small[:, L_WI + h:L_WI + h + 1] * jnp.maximum(d, 0.0)
        sc_scr[:, pl.ds(off, sc)] = jnp.where(off + lane_k <= qpos, acc, NEG_INF)
        return carry

    lax.fori_loop(0, nkc * (kc // sc), score_body, 0)

    _topk_mask_inplace(sc_scr, nkc, kc, n_sel)

    pair = lambda ref, h, off: ref[0, pl.ds(off, kc), (h // 2) * LANES:(h // 2 + 1) * LANES]
    outs = []
    for h0 in range(0, N_HEADS, HEAD_GROUP):
        heads = list(range(h0, h0 + HEAD_GROUP))
        outs += _two_pass_attention(
            nkc, kc, heads, [_pair_masked(qa, h) for h in heads],
            functools.partial(pair, ka_ref), functools.partial(pair, va_ref),
            lambda h, off: sc_scr[:, pl.ds(off, kc)], s_scr)
    _store_pairs(o_ref, outs)


def _dsa_prompt(qi, small, qa, kid, kab, vab):
    b, t, _ = qa.shape
    tq = min(DSA_TQ, t)
    n_sel = min(TOPK_MAX, t // 4)
    rowq = lambda w: pl.BlockSpec((1, tq, w), lambda bi, i: (bi, i, 0))
    full = lambda w: pl.BlockSpec((1, t, w), lambda bi, i: (bi, 0, 0))
    return pl.pallas_call(
        functools.partial(_dsa_prompt_kernel, n_sel=n_sel),
        grid=(b, t // tq),
        in_specs=[rowq(HD), rowq(LANES), rowq(HD), full(LANES), full(HD), full(HD)],
        out_specs=rowq(HD),
        out_shape=jax.ShapeDtypeStruct((b, t, HD), BF16),
        scratch_shapes=[pltpu.VMEM((tq, t), F32), pltpu.VMEM((HEAD_GROUP, tq, t), F32)],
        compiler_params=_cparams(("arbitrary", "arbitrary")),
        name="dsa_prompt",
    )(qi, small, qa, kid, kab, vab)


FOX_TQ = 256
FOX_KC = 512


def _fox_prompt_kernel(qb_ref, small_ref, drow_ref, kb_ref, vb_ref, o_ref, s_scr):
    qt = pl.program_id(1)
    tq = qb_ref.shape[1]
    kc = FOX_KC
    nkc = ((qt + 1) * tq + kc - 1) // kc
    small = small_ref[0]
    qb = qb_ref[0]
    qpos = qt * tq + lax.broadcasted_iota(I32, (tq, kc), 0)
    lane_k = lax.broadcasted_iota(I32, (tq, kc), 1)
    pair = lambda ref, h, off: ref[0, pl.ds(off, kc), (h // 2) * LANES:(h // 2 + 1) * LANES]

    def bias(h, off):
        b = (small[:, L_D + h:L_D + h + 1] - drow_ref[0, h:h + 1, pl.ds(off, kc)]) * LOG2E
        return jnp.where(off + lane_k <= qpos, b, NEG_INF)

    outs = []
    for h0 in range(0, N_HEADS, HEAD_GROUP):
        heads = list(range(h0, h0 + HEAD_GROUP))
        outs += _two_pass_attention(
            nkc, kc, heads, [_pair_masked(qb, h) for h in heads],
            functools.partial(pair, kb_ref), functools.partial(pair, vb_ref), bias, s_scr)
    _store_pairs(o_ref, outs)


def _fox_prompt(qb, small, drow, kbb, vbb):
    b, t, _ = qb.shape
    tq = min(FOX_TQ, t)
    rowq = lambda w: pl.BlockSpec((1, tq, w), lambda bi, i: (bi, i, 0))
    full = lambda w: pl.BlockSpec((1, t, w), lambda bi, i: (bi, 0, 0))
    return pl.pallas_call(
        _fox_prompt_kernel,
        grid=(b, t // tq),
        in_specs=[rowq(HD), rowq(LANES), pl.BlockSpec((1, N_HEADS, t), lambda bi, i: (bi, 0, 0)),
                  full(HD), full(HD)],
        out_specs=rowq(HD),
        out_shape=jax.ShapeDtypeStruct((b, t, HD), BF16),
        scratch_shapes=[pltpu.VMEM((HEAD_GROUP, tq, t), F32)],
        compiler_params=_cparams(("arbitrary", "arbitrary")),
        name="fox_prompt",
    )(qb, small, drow, kbb, vbb)


def _mix_kernel(x_ref, oa_ref, ob_ref, ga_ref, gb_ref, gt1_ref, sh2_ref, sc2_ref, gn2_ref,
                woa_ref, wob_ref, wout_ref, wr_ref, br_ref, x1_o, h2_o, comb_o):
    ya = _dot(oa_ref[0], woa_ref[...])
    yb = _dot(ob_ref[0], wob_ref[...])
    mix = ga_ref[0].astype(F32) * ya + gb_ref[0].astype(F32) * yb
    x1 = x_ref[0] + gt1_ref[0] * _dot(mix.astype(BF16), wout_ref[...])
    x1_o[0] = x1
    ms = jnp.mean(x1 * x1, axis=-1, keepdims=True)
    h2 = (x1 * lax.rsqrt(ms + EPS) * gn2_ref[...]) * (1.0 + sc2_ref[0]) + sh2_ref[0]
    h2b = h2.astype(BF16)
    h2_o[0] = h2b
    wr = wr_ref[...]
    w_hi = wr.astype(BF16)
    w_lo = (wr - w_hi.astype(F32)).astype(BF16)
    h_lo = (h2 - h2b.astype(F32)).astype(BF16)
    logits = _dot(h2b, w_hi) + _dot(h2b, w_lo) + _dot(h_lo, w_hi) + br_ref[...]
    lane = lax.broadcasted_iota(I32, logits.shape, 1)
    work = logits
    vals, idxs = [], []
    for _ in range(TOP_K_EXPERTS):
        mx = jnp.max(work, axis=1, keepdims=True)
        idx = jnp.min(jnp.where(work == mx, lane, LANES), axis=1, keepdims=True)
        vals.append(mx)
        idxs.append(idx)
        work = jnp.where(lane == idx, NEG_INF, work)
    es = [jnp.exp(v - vals[0]) for v in vals]
    denom = es[0] + es[1] + es[2] + es[3]
    comb = jnp.zeros(logits.shape, F32)
    for e, idx in zip(es, idxs):
        comb = comb + jnp.where(lane == idx, e / denom, 0.0)
    comb_o[0] = comb


def _mix(x, oa, ob, ga, gb, gt1, sh2, sc2, g_norm2, mw, tm):
    woa, wob, wout, wr, br = mw
    b, t, d = x.shape
    per_row = gt1.shape[1] != 1
    mod_spec = (pl.BlockSpec((1, tm, d), lambda bi, i: (bi, i, 0)) if per_row
                else pl.BlockSpec((1, 1, d), lambda bi, i: (bi, 0, 0)))
    const = lambda shape: pl.BlockSpec(shape, lambda bi, i: tuple(0 for _ in shape))
    row = lambda width: pl.BlockSpec((1, tm, width), lambda bi, i: (bi, i, 0))
    return pl.pallas_call(
        _mix_kernel,
        grid=(b, t // tm),
        in_specs=[row(d), row(HD), row(HD), row(d), row(d), mod_spec, mod_spec, mod_spec,
                  const((1, d)), const(woa.shape), const(wob.shape), const(wout.shape),
                  const(wr.shape), const(br.shape)],
        out_specs=[row(d), row(d), row(LANES)],
        out_shape=[jax.ShapeDtypeStruct((b, t, d), F32), jax.ShapeDtypeStruct((b, t, d), BF16),
                   jax.ShapeDtypeStruct((b, t, LANES), F32)],
        compiler_params=_cparams(("arbitrary", "arbitrary")),
        name="mix_router",
    )(x, oa, ob, ga, gb, gt1, sh2, sc2, g_norm2.reshape(1, d), woa, wob, wout, wr, br)


MOE_TM = 1024
MOE_CH = 160
MOE_RB = 256
MOE_WSPLIT = 2


def _moe_kernel(h_ref, comb_ref, x1_ref, gt2_ref, *refs):
    wup_refs, refs = refs[:2 * MOE_WSPLIT], refs[2 * MOE_WSPLIT:]
    bup_ref, refs = refs[0], refs[1:]
    wdn_refs, refs = refs[:MOE_WSPLIT], refs[MOE_WSPLIT:]
    bdn_ref, y_o, rank_t_ref, comb_t_ref = refs
    wb = D_FF // MOE_WSPLIT
    e = pl.program_id(2)
    tm = h_ref.shape[1]
    ch = MOE_CH
    rb = min(MOE_RB, tm)

    @pl.when(e == 0)
    def _():
        y_o[0] = jnp.zeros(y_o.shape[1:], F32)
        comb = comb_ref[0]
        sel = jnp.where(comb > 0.0, 1.0, 0.0).astype(BF16)
        eye = jnp.where(lax.broadcasted_iota(I32, (LANES, LANES), 0)
                        == lax.broadcasted_iota(I32, (LANES, LANES), 1), 1.0, 0.0).astype(BF16)
        sel_t = _dot_nt(eye, sel).astype(BF16)
        hi, mid, lo = _split3(comb)
        comb_t_ref[...] = _dot_nt(eye, hi) + _dot_nt(eye, mid) + _dot_nt(eye, lo)
        for blk in range(tm // rb):
            r_j = lax.broadcasted_iota(I32, (tm, rb), 0)
            c_j = blk * rb + lax.broadcasted_iota(I32, (tm, rb), 1)
            earlier_t = jnp.where(r_j < c_j, 1.0, 0.0).astype(BF16)
            rank_t_ref[:, blk * rb:(blk + 1) * rb] = _dot(sel_t, earlier_t)

    rank_row = rank_t_ref[pl.ds(e, 1), :]
    c_row = comb_t_ref[pl.ds(e, 1), :]
    n_tok = jnp.sum(jnp.where(c_row > 0.0, 1.0, 0.0)).astype(I32)

    def chunk(c, carry):
        base = (c * ch).astype(F32)
        slot_r = base + lax.broadcasted_iota(I32, (ch, tm), 0).astype(F32)
        pf = jnp.where((rank_row == slot_r) & (c_row > 0.0), 1.0, 0.0)
        c_sorted = jnp.sum(pf * c_row, axis=1, keepdims=True)
        xg = _dot(pf.astype(BF16), h_ref[0]).astype(BF16)
        ye = jnp.zeros((ch, y_o.shape[2]), F32)
        for j in range(MOE_WSPLIT):
            bg = bup_ref[0, :, j * wb:(j + 1) * wb]
            bl = bup_ref[0, :, D_FF + j * wb:D_FF + (j + 1) * wb]
            g = jnp.minimum(_dot(xg, wup_refs[j][0]) + bg, SWIGLU_LIMIT)
            lin = jnp.clip(_dot(xg, wup_refs[MOE_WSPLIT + j][0]) + bl, -SWIGLU_LIMIT, SWIGLU_LIMIT)
            act = (lin + 1.0) * g * _sigmoid(SWIGLU_ALPHA * g)
            ye = ye + _dot(act.astype(BF16), wdn_refs[j][0])
        yw = (c_sorted * (ye + bdn_ref[0])).astype(BF16)
        y_o[0] += _dot(pf.T.astype(BF16), yw)
        return carry

    lax.fori_loop(0, (n_tok + ch - 1) // ch, chunk, 0)

    @pl.when(e == pl.num_programs(2) - 1)
    def _():
        y_o[0] = x1_ref[0] + gt2_ref[0] * y_o[0]


def _moe(h2, comb, x1, gt2, ew, tm):
    wup, bup, wdn, bdn = ew
    b, t, d = x1.shape
    ne = wup.shape[0]
    wb = D_FF // MOE_WSPLIT
    per_row = gt2.shape[1] != 1
    mod_spec = (pl.BlockSpec((1, tm, d), lambda bi, i, e: (bi, i, 0)) if per_row
                else pl.BlockSpec((1, 1, d), lambda bi, i, e: (bi, 0, 0)))
    row = lambda width: pl.BlockSpec((1, tm, width), lambda bi, i, e: (bi, i, 0))
    return pl.pallas_call(
        _moe_kernel,
        grid=(b, t // tm, ne),
        in_specs=[row(d), row(LANES), row(d), mod_spec]
                 + [pl.BlockSpec((1, d, wb), functools.partial(lambda j, bi, i, e: (e, 0, j), j))
                    for j in range(2 * MOE_WSPLIT)]
                 + [pl.BlockSpec((1, 1, 2 * D_FF), lambda bi, i, e: (e, 0, 0))]
                 + [pl.BlockSpec((1, wb, d), functools.partial(lambda j, bi, i, e: (e, j, 0), j))
                    for j in range(MOE_WSPLIT)]
                 + [pl.BlockSpec((1, 1, d), lambda bi, i, e: (e, 0, 0))],
        out_specs=row(d),
        out_shape=jax.ShapeDtypeStruct((b, t, d), F32),
        scratch_shapes=[pltpu.VMEM((LANES, tm), F32), pltpu.VMEM((LANES, tm), F32)],
        compiler_params=_cparams(("arbitrary", "arbitrary", "arbitrary")),
        name="moe",
    )(h2, comb, x1, gt2, *([wup] * (2 * MOE_WSPLIT)), bup, *([wdn] * MOE_WSPLIT), bdn)


def _mod_split(mod, lo, hi):
    return jnp.split(mod[lo:hi], N_ADA, axis=-1)


def _prompt_group(x, mods, pw, mw, ew, g_norm1, g_norm2):
    b, t, d = x.shape
    sh1, sc1, gt1, sh2, sc2, gt2 = [m[:, None, :] for m in mods]
    tables = _rope_tables(jnp.arange(t, dtype=I32))
    (ka, va, kb, vb, small, qa, qi, qb, kab, vab, kbb, vbb, kid, ga, gb) = _proj(
        x, g_norm1, sh1, sc1, pw, tables, tm=min(256, t))
    out_a = _dsa_prompt(qi, small, qa, kid, kab, vab)
    drow = jnp.swapaxes(small[:, :, L_D:L_D + N_HEADS], 1, 2)
    out_b = _fox_prompt(qb, small, drow, kbb, vbb)
    x1, h2, comb = _mix(x, out_a, out_b, ga, gb, gt1, sh2, sc2, g_norm2, mw, tm=min(256, t))
    y = _moe(h2, comb, x1, gt2, ew, tm=min(MOE_TM, t))
    heads = lambda z: z.reshape(1, b, t, N_HEADS, HEAD_DIM)
    return (y, heads(ka), heads(va), small[None, :, :, :D_IDX], heads(kb), heads(vb),
            small[None, :, :, L_LF:L_D])


def _mix_weights(w_oa, w_ob, w_out, w_router, b_router):
    ne = w_router.shape[1]
    wr = jnp.concatenate([w_router, jnp.zeros((w_router.shape[0], LANES - ne), w_router.dtype)], axis=1)
    br = jnp.concatenate([b_router, jnp.full((LANES - ne,), NEG_INF, b_router.dtype)]).reshape(1, LANES)
    return w_oa.astype(BF16), w_ob.astype(BF16), w_out.astype(BF16), wr, br


def _moe_weights(w_up, b_up, w_down, b_down):
    ne = w_up.shape[0]
    return (w_up.astype(BF16), b_up.reshape(ne, 1, -1), w_down.astype(BF16), b_down.reshape(ne, 1, -1))


SCORE_CHUNK = 1024


def _page_gather(pt_ref, b, cache_ref, dst_slab, sem):
    npages = pt_ref.shape[1]

    def copy(p, page):
        return pltpu.make_async_copy(cache_ref.at[page], dst_slab(p), sem)

    def issue(p, c):
        copy(p, pt_ref[b, p]).start()
        return c

    def wait(p, c):
        copy(p, 0).wait()
        return c

    lax.fori_loop(0, npages, issue, 0)
    lax.fori_loop(0, npages, wait, 0)


def _sidx_kernel(pt_ref, q8_ref, w8_ref, knew_ref, cache_ref, sc_ref, kbuf, sem):
    b = pl.program_id(0)
    _page_gather(pt_ref, b, cache_ref,
                 lambda p: kbuf.at[:, pl.ds(pl.multiple_of(p * PAGE_SIZE, PAGE_SIZE), PAGE_SIZE)], sem)
    n_past = kbuf.shape[1]
    q8 = q8_ref[0]
    w8 = w8_ref[0]
    q_hi = q8.astype(BF16)
    q_lo = (q8 - q_hi.astype(F32)).astype(BF16)
    for c in range(n_past // SCORE_CHUNK):
        kf = kbuf[:, c * SCORE_CHUNK:(c + 1) * SCORE_CHUNK]
        k_hi = kf.astype(BF16)
        k_lo = (kf - k_hi.astype(F32)).astype(BF16)
        d = _dot(q_hi, k_hi) + _dot(q_hi, k_lo) + _dot(q_lo, k_hi)
        sc_ref[0, :, c * SCORE_CHUNK:(c + 1) * SCORE_CHUNK] = jnp.sum(
            w8 * jnp.maximum(d, 0.0), axis=0, keepdims=True)
    dn = jnp.sum(q8 * knew_ref[0], axis=1, keepdims=True)
    sn = jnp.sum(w8 * jnp.maximum(dn, 0.0), axis=0, keepdims=True)
    lane = lax.broadcasted_iota(I32, (1, LANES), 1)
    sc_ref[0, :, n_past:n_past + LANES] = jnp.where(lane == 0, sn, NEG_INF)


def _sample_scores(page_table, q8, w8, knew, cache_kidx):
    nb, npages = page_table.shape
    n_past = npages * PAGE_SIZE
    grid_spec = pltpu.PrefetchScalarGridSpec(
        num_scalar_prefetch=1,
        grid=(nb,),
        in_specs=[pl.BlockSpec((1, N_HEADS, D_IDX), lambda b, pt: (b, 0, 0)),
                  pl.BlockSpec((1, N_HEADS, 1), lambda b, pt: (b, 0, 0)),
                  pl.BlockSpec((1, 1, D_IDX), lambda b, pt: (b, 0, 0)),
                  pl.BlockSpec(memory_space=pl.ANY)],
        out_specs=pl.BlockSpec((1, 1, n_past + LANES), lambda b, pt: (b, 0, 0)),
        scratch_shapes=[pltpu.VMEM((D_IDX, n_past), F32), pltpu.SemaphoreType.DMA(())],
    )
    return pl.pallas_call(
        _sidx_kernel,
        grid_spec=grid_spec,
        out_shape=jax.ShapeDtypeStruct((nb, 1, n_past + LANES), F32),
        compiler_params=_cparams(("arbitrary",)),
        name="sample_scores",
    )(page_table, q8, w8, knew, cache_kidx)


def _ssel_kernel(sc_ref, o_ref, *, n_sel):
    o_ref[...] = sc_ref[...]
    _topk_mask_inplace(o_ref, o_ref.shape[1] // LANES, LANES, n_sel)


def _sample_select(scores, n_keys):
    rows, n = scores.shape
    n_sel = min(TOPK_MAX, n_keys // 4)
    return pl.pallas_call(
        functools.partial(_ssel_kernel, n_sel=n_sel),
        out_shape=jax.ShapeDtypeStruct((rows, n), F32),
        compiler_params=pltpu.CompilerParams(vmem_limit_bytes=VMEM_LIMIT),
        name="sample_select",
    )(scores)


def _split3(x):
    hi = x.astype(BF16)
    r1 = x - hi.astype(F32)
    mid = r1.astype(BF16)
    lo = (r1 - mid.astype(F32)).astype(BF16)
    return hi, mid, lo


def _dot3_l(x, m):
    hi, mid, lo = _split3(x)
    return _dot(hi, m) + _dot(mid, m) + _dot(lo, m)


def _dot3_r(m, x):
    hi, mid, lo = _split3(x)
    return _dot(m, hi) + _dot(m, mid) + _dot(m, lo)


def _fbias_kernel(pt_ref, lfnew_ref, cache_ref, o_ref, lbuf, sem):
    b = pl.program_id(0)
    _page_gather(pt_ref, b, cache_ref, lambda p: lbuf.at[p], sem)
    npages = lbuf.shape[0]
    r_i = lax.broadcasted_iota(I32, (npages, npages), 0)
    c_i = lax.broadcasted_iota(I32, (npages, npages), 1)
    later_pages = jnp.where(c_i > r_i, 1.0, 0.0).astype(BF16)
    t_r = lax.broadcasted_iota(I32, (PAGE_SIZE, PAGE_SIZE), 0)
    t_c = lax.broadcasted_iota(I32, (PAGE_SIZE, PAGE_SIZE), 1)
    later_toks = jnp.where(t_r > t_c, 1.0, 0.0).astype(BF16)
    lfnew = lfnew_ref[0]
    for h in range(N_HEADS):
        x = lbuf[:, h, :]
        within = _dot3_l(x, later_toks)
        tot = jnp.sum(x, axis=1, keepdims=True)
        later = _dot3_r(later_pages, jnp.broadcast_to(tot, (npages, PAGE_SIZE)))
        o_ref[0, h] = (within + later + lfnew[:, h:h + 1]) * LOG2E


def _sample_fox_bias(page_table, lfnew, cache_logf_t):
    nb, npages = page_table.shape
    grid_spec = pltpu.PrefetchScalarGridSpec(
        num_scalar_prefetch=1,
        grid=(nb,),
        in_specs=[pl.BlockSpec((1, 1, N_HEADS), lambda b, pt: (b, 0, 0)),
                  pl.BlockSpec(memory_space=pl.ANY)],
        out_specs=pl.BlockSpec((1, N_HEADS, npages, PAGE_SIZE), lambda b, pt: (b, 0, 0, 0)),
        scratch_shapes=[pltpu.VMEM((npages, N_HEADS, PAGE_SIZE), F32), pltpu.SemaphoreType.DMA(())],
    )
    return pl.pallas_call(
        _fbias_kernel,
        grid_spec=grid_spec,
        out_shape=jax.ShapeDtypeStruct((nb, N_HEADS, npages, PAGE_SIZE), F32),
        compiler_params=_cparams(("arbitrary",)),
        name="sample_fox_bias",
    )(page_table, lfnew, cache_logf_t)


DEC_PAGES = 16
QROWS = 16


def _decode_kernel(pt_ref, q_ref, knew_ref, vnew_ref, bnew_ref, *refs, npg):
    k_refs, v_refs, b_refs = refs[:npg], refs[npg:2 * npg], refs[2 * npg:3 * npg]
    o_ref, m_scr, l_scr, acc_scr = refs[3 * npg:]
    s = pl.program_id(1)
    hrow = lax.broadcasted_iota(I32, (QROWS, HD), 0)
    lane = lax.broadcasted_iota(I32, (QROWS, HD), 1)
    own = (lane // HEAD_DIM) == hrow
    qf = jnp.where(own, jnp.broadcast_to(q_ref[0].astype(F32), (QROWS, HD)), 0.0)
    qm = qf.astype(BF16)

    @pl.when(s == 0)
    def _():
        sn = jnp.sum(qf[:N_HEADS] * knew_ref[0], axis=1, keepdims=True) + bnew_ref[0]
        live = sn > NEG_INF
        m_scr[...] = jnp.broadcast_to(sn, m_scr.shape)
        l_scr[...] = jnp.broadcast_to(jnp.where(live, 1.0, 0.0), l_scr.shape)
        acc_scr[...] = jnp.where(live, 1.0, 0.0) * jnp.broadcast_to(vnew_ref[0], acc_scr.shape)

    sc = jnp.concatenate(
        [_dot(qm, k_refs[j][0].astype(BF16))[:N_HEADS] + b_refs[j][0, 0] for j in range(npg)], axis=1)
    m_old = m_scr[...]
    m_new = jnp.maximum(m_old, jnp.max(sc, axis=1, keepdims=True))
    m_safe = jnp.where(m_new == NEG_INF, 0.0, m_new)
    p = jnp.exp2(sc - m_safe[:, 0:1])
    alpha = jnp.exp2(m_old - m_safe)
    l_scr[...] = alpha * l_scr[...] + jnp.sum(p, axis=1, keepdims=True)
    p16 = jnp.concatenate([p, jnp.zeros_like(p)], axis=0).astype(BF16)
    pv = _dot_nt(p16[:, 0:PAGE_SIZE], v_refs[0][0].astype(BF16))
    for j in range(1, npg):
        pv = pv + _dot_nt(p16[:, j * PAGE_SIZE:(j + 1) * PAGE_SIZE], v_refs[j][0].astype(BF16))
    acc_scr[...] = alpha[:, 0:1] * acc_scr[...] + pv[:N_HEADS]
    m_scr[...] = m_new

    @pl.when(s == pl.num_programs(1) - 1)
    def _():
        o = jnp.where(own[:N_HEADS], acc_scr[...] / l_scr[:, 0:1], 0.0)
        o_ref[0] = jnp.sum(o, axis=0, keepdims=True)


def _paged_decode(page_table, q, knew, vnew, bnew, cache_k, cache_v, bias):
    nb, npages = page_table.shape
    hb = bias.shape[2]
    npg = min(DEC_PAGES, npages)
    kv_spec = lambda j: pl.BlockSpec((1, HD, PAGE_SIZE), lambda b, s, pt: (pt[b, s * npg + j], 0, 0))
    b_spec = lambda j: pl.BlockSpec((1, 1, hb, PAGE_SIZE), lambda b, s, pt: (b, s * npg + j, 0, 0))
    row = lambda w: pl.BlockSpec((1, 1, w), lambda b, s, pt: (b, 0, 0))
    grid_spec = pltpu.PrefetchScalarGridSpec(
        num_scalar_prefetch=1,
        grid=(nb, npages // npg),
        in_specs=[row(HD), row(HD), row(HD), pl.BlockSpec((1, N_HEADS, 1), lambda b, s, pt: (b, 0, 0))]
                 + [kv_spec(j) for j in range(npg)] * 2 + [b_spec(j) for j in range(npg)],
        out_specs=row(HD),
        scratch_shapes=[pltpu.VMEM((N_HEADS, LANES), F32), pltpu.VMEM((N_HEADS, LANES), F32),
                        pltpu.VMEM((N_HEADS, HD), F32)],
    )
    return pl.pallas_call(
        functools.partial(_decode_kernel, npg=npg),
        grid_spec=grid_spec,
        out_shape=jax.ShapeDtypeStruct((nb, 1, HD), F32),
        compiler_params=_cparams(("arbitrary", "arbitrary")),
        name="paged_decode",
    )(page_table, q, knew, vnew, bnew, *([cache_k] * npg), *([cache_v] * npg), *([bias] * npg))


def _sample_group(x, mods, pw, mw, ew, g_norm1, g_norm2, caches, page_table, pos):
    nb, t, d = x.shape
    cache_k_a, cache_v_a, cache_kidx, cache_k_b, cache_v_b, cache_logf = caches
    n_pool = cache_k_a.shape[0]
    npages = page_table.shape[1]
    n_past = npages * PAGE_SIZE
    sh1, sc1, gt1, sh2, sc2, gt2 = [m[None] for m in mods]
    xs = x.reshape(1, nb, d)
    tables = _rope_tables(jnp.full((nb,), pos, I32))
    (ka, va, kb, vb, small, qa, qi, qb, _, _, _, _, _, ga, gb) = _proj(
        xs, g_norm1, sh1, sc1, pw, tables, tm=nb, qi_dtype=F32)
    col = lambda z: z.reshape(nb, 1, -1)
    kv_t = lambda c: jnp.transpose(c, (0, 2, 3, 1)).reshape(n_pool, HD, PAGE_SIZE)
    scores = _sample_scores(page_table, qi.reshape(nb, N_HEADS, D_IDX),
                            small[0, :, L_WI:L_LF].reshape(nb, N_HEADS, 1),
                            small[0, :, :D_IDX].reshape(nb, 1, D_IDX), jnp.swapaxes(cache_kidx, 1, 2))
    mask = _sample_select(scores.reshape(nb, -1), n_past + t)
    bias_a = mask[:, :n_past].reshape(nb, npages, 1, PAGE_SIZE)
    bnew_a = jnp.broadcast_to(mask[:, n_past:n_past + 1, None], (nb, N_HEADS, 1))
    out_a = _paged_decode(page_table, col(qa), col(ka), col(va), bnew_a,
                          kv_t(cache_k_a), kv_t(cache_v_a), bias_a)
    logf = small[0, :, L_LF:L_D]
    bias_b = _sample_fox_bias(page_table, logf.reshape(nb, 1, N_HEADS), jnp.swapaxes(cache_logf, 1, 2))
    out_b = _paged_decode(page_table, col(qb), col(kb), col(vb), jnp.zeros((nb, N_HEADS, 1), F32),
                          kv_t(cache_k_b), kv_t(cache_v_b), jnp.swapaxes(bias_b, 1, 2))
    oa = out_a.reshape(1, nb, HD).astype(BF16)
    ob = out_b.reshape(1, nb, HD).astype(BF16)
    x1, h2, comb = _mix(xs, oa, ob, ga, gb, gt1, sh2, sc2, g_norm2, mw, tm=nb)
    rows = -(-nb // MOE_RB) * MOE_RB
    padr = lambda z: jnp.pad(z, ((0, 0), (0, rows - nb), (0, 0)))
    y = _moe(padr(h2), padr(comb), padr(x1), padr(gt2), ew, tm=rows)[:, :nb]
    heads = lambda z: z.reshape(1, nb, t, N_HEADS, HEAD_DIM)
    return (y.reshape(nb, t, d), heads(ka), heads(va), small[0, :, :D_IDX].reshape(1, nb, t, D_IDX),
            heads(kb), heads(vb), logf.reshape(1, nb, t, N_HEADS))


def kernel(x_prompt, x_sample, c_prompt, c_sample, cache_k_a, cache_v_a, cache_kidx_a, cache_k_b,
           cache_v_b, cache_logf_b, page_table, w_ada, b_ada, g_norm1, w_in, b_f, g_qa, g_ka, g_kidx,
           g_qb, g_kb, w_oa, w_ob, w_out, g_norm2, w_router, b_router, w_up, b_up, w_down, b_down):
    depth = w_in.shape[0]
    assert depth == 1, "single-layer trunk"
    l = 0
    nbp, nbs = x_prompt.shape[0], x_sample.shape[0]
    n_past = page_table.shape[1] * PAGE_SIZE
    rows = nbp + nbs
    pad = (-rows) % 8
    c_all = jnp.concatenate([c_prompt, c_sample, jnp.zeros((pad, c_prompt.shape[1]), F32)], axis=0)
    mod = _adaln(c_all, w_ada[l], b_ada[l])
    pw = _proj_weights(w_in[l], b_f[l], g_qa[l], g_ka[l], g_kidx[l], g_qb[l], g_kb[l])
    mw = _mix_weights(w_oa[l], w_ob[l], w_out[l], w_router[l], b_router[l])
    ew = _moe_weights(w_up[l], b_up[l], w_down[l], b_down[l])
    outs_p = _prompt_group(x_prompt, _mod_split(mod, 0, nbp), pw, mw, ew, g_norm1[l], g_norm2[l])
    caches = (cache_k_a[l], cache_v_a[l], cache_kidx_a[l], cache_k_b[l], cache_v_b[l], cache_logf_b[l])
    outs_s = _sample_group(x_sample, _mod_split(mod, nbp, rows), pw, mw, ew, g_norm1[l], g_norm2[l],
                           caches, page_table, n_past)
    return (outs_p[0], outs_s[0]) + tuple(outs_p[1:]) + tuple(outs_s[1:])
```

```python
import functools

import numpy as np
import jax
import jax.numpy as jnp
from jax import lax
from jax.experimental import pallas as pl
from jax.experimental.pallas import tpu as pltpu

F32 = jnp.float32
BF16 = jnp.bfloat16
I32 = jnp.int32

D_MODEL = 1024
HEAD_DIM = 64
N_HEADS = 8
HD = N_HEADS * HEAD_DIM
D_IDX = 64
ROT_DIM = HEAD_DIM // 4
ROPE_THETA = 500000.0
TOPK_MAX = 256
N_EXPERTS = 32
TOP_K_EXPERTS = 4
D_FF = D_MODEL
SWIGLU_LIMIT = 7.0
SWIGLU_ALPHA = 1.702
EPS = 1e-6
N_ADA = 6
PAGE_SIZE = 128
LANES = 128
W_IDX_SCALE = N_HEADS ** -0.5 * D_IDX ** -0.5
LOG2E = float(np.log2(np.e))
QK_SCALE = HEAD_DIM ** -0.5 * LOG2E

C_QA, C_KA, C_VA, C_QI, C_QB, C_KB, C_VB = (i * HD for i in range(7))
C_GA = 7 * HD
C_GB = C_GA + D_MODEL
C_SM = C_GB + D_MODEL
D_IN_PAD = C_SM + LANES
L_WI = D_IDX
L_LF = D_IDX + N_HEADS
L_D = L_LF + N_HEADS

NEG_INF = float("-inf")
INT_MIN = -2 ** 31
KEY_NEG_INF = int(np.int32(np.uint32(0xFF800000) ^ np.uint32(0x7FFFFFFF)))

VMEM_LIMIT = 56 * 1024 * 1024


def _cparams(sem):
    return pltpu.CompilerParams(dimension_semantics=sem, vmem_limit_bytes=VMEM_LIMIT)


def _sigmoid(x):
    return 1.0 / (1.0 + jnp.exp(-x))


def _dot(a, b):
    return jnp.dot(a, b, preferred_element_type=F32)


def _dot_nt(a, b):
    return lax.dot_general(a, b, (((1,), (1,)), ((), ())), preferred_element_type=F32)


def _adaln_kernel(c_ref, w_ref, b_ref, o_ref):
    c = c_ref[...]
    s = c * _sigmoid(c)
    o_ref[...] = _dot(s.astype(BF16), w_ref[...].astype(BF16)) + b_ref[...]


def _adaln(c, w_ada, b_ada):
    rows, d = c.shape
    n = w_ada.shape[1]
    tn = 1536
    return pl.pallas_call(
        _adaln_kernel,
        grid=(n // tn,),
        in_specs=[pl.BlockSpec((rows, d), lambda j: (0, 0)),
                  pl.BlockSpec((d, tn), lambda j: (0, j)),
                  pl.BlockSpec((1, tn), lambda j: (0, j))],
        out_specs=pl.BlockSpec((rows, tn), lambda j: (0, j)),
        out_shape=jax.ShapeDtypeStruct((rows, n), F32),
        compiler_params=_cparams(("arbitrary",)),
        name="adaln",
    )(c, w_ada, b_ada.reshape(1, n))


def _rope(z, cos, sp, sm):
    w = z.shape[1]
    return z * cos + pltpu.roll(z, 8, 1) * sp + pltpu.roll(z, w - 8, 1) * sm


def _tile4(t):
    return jnp.concatenate([t, t, t, t], axis=1)


def _proj_kernel(x_ref, gn_ref, sh_ref, sc_ref, w_ref, gains_ref, gsm_ref, bf_ref,
                 cos_ref, sp_ref, sm_ref, gmat_ref, gsmat_ref,
                 ka_o, va_o, kb_o, vb_o, small_o,
                 qa_o, qi_o, qb_o, kab_o, vab_o, kbb_o, vbb_o, kid_o, ga_o, gb_o,
                 carry_ref):
    i = pl.program_id(1)
    tm = x_ref.shape[1]

    @pl.when(i == 0)
    def _():
        carry_ref[...] = jnp.zeros_like(carry_ref)

    x = x_ref[0]
    ms = jnp.mean(x * x, axis=-1, keepdims=True)
    y = x * lax.rsqrt(ms + EPS) * gn_ref[...]
    h = y * (1.0 + sc_ref[0]) + sh_ref[0]
    hb = h.astype(BF16)

    cos128, sp128, sm128 = cos_ref[...], sp_ref[...], sm_ref[...]
    cos, sp, sm = _tile4(cos128), _tile4(sp128), _tile4(sm128)
    gmat = gmat_ref[...]

    def sect(c0, width=HD):
        return _dot(hb, w_ref[:, c0:c0 + width])

    def headnorm(z, g):
        msq = _dot((z * z).astype(BF16), gmat)
        return z * lax.rsqrt(msq + EPS) * g

    qa = _rope(headnorm(sect(C_QA), gains_ref[0:1, :]), cos, sp, sm)
    qa_o[0] = (qa * QK_SCALE).astype(BF16)
    ka = _rope(headnorm(sect(C_KA), gains_ref[1:2, :]), cos, sp, sm)
    ka_o[0] = ka
    kab_o[0] = ka.astype(BF16)
    va = sect(C_VA)
    va_o[0] = va
    vab_o[0] = va.astype(BF16)
    qi_o[0] = _rope(sect(C_QI), cos, sp, sm).astype(qi_o.dtype)
    qb_o[0] = (headnorm(sect(C_QB), gains_ref[2:3, :]) * QK_SCALE).astype(BF16)
    kb = headnorm(sect(C_KB), gains_ref[3:4, :])
    kb_o[0] = kb
    kbb_o[0] = kb.astype(BF16)
    vb = sect(C_VB)
    vb_o[0] = vb
    vbb_o[0] = vb.astype(BF16)
    ga_o[0] = _sigmoid(sect(C_GA, D_MODEL)).astype(BF16)
    gb_o[0] = _sigmoid(sect(C_GB, D_MODEL)).astype(BF16)

    zs = sect(C_SM, LANES)
    lane = lax.broadcasted_iota(I32, (tm, LANES), 1)
    msq = _dot((zs * zs).astype(BF16), gsmat_ref[...])
    ki = _rope(zs * lax.rsqrt(msq + EPS) * gsm_ref[...], cos128, sp128, sm128)
    kid_o[0] = (ki + pltpu.roll(ki, D_IDX, 1)).astype(BF16)
    t = zs + bf_ref[...]
    logf = jnp.minimum(t, 0.0) - jnp.log1p(jnp.exp(-jnp.abs(t)))
    lf_only = jnp.where((lane >= L_LF) & (lane < L_D), logf, 0.0)
    r_i = lax.broadcasted_iota(I32, (tm, tm), 0)
    c_i = lax.broadcasted_iota(I32, (tm, tm), 1)
    tri = (c_i <= r_i).astype(F32)
    dcum = jnp.dot(tri, lf_only, preferred_element_type=F32,
                   precision=lax.Precision.HIGHEST) + carry_ref[...]
    carry_ref[...] = dcum[tm - 1:tm, :]
    small = (ki + jnp.where((lane >= L_WI) & (lane < L_LF), zs * W_IDX_SCALE, 0.0)
             + lf_only + pltpu.roll(dcum, N_HEADS, 1))
    small_o[0] = small


def _rope_tables(pos):
    half = ROT_DIM // 2
    inv_freq = ROPE_THETA ** (-jnp.arange(half, dtype=F32) * 2.0 / ROT_DIM)
    ang = pos.astype(F32)[:, None] * inv_freq[None, :]
    cos8, sin8 = jnp.cos(ang), jnp.sin(ang)
    t = pos.shape[0]
    one = jnp.ones((t, HEAD_DIM - ROT_DIM), F32)
    zero = jnp.zeros((t, HEAD_DIM - ROT_DIM), F32)
    z8 = jnp.zeros((t, half), F32)
    cos64 = jnp.concatenate([cos8, cos8, one], axis=1)
    sp64 = jnp.concatenate([z8, sin8, zero], axis=1)
    sm64 = jnp.concatenate([-sin8, z8, zero], axis=1)
    dup = lambda a: jnp.concatenate([a, a], axis=1)
    return dup(cos64), dup(sp64), dup(sm64)


def _proj_weights(w_in, b_f, g_qa, g_ka, g_kidx, g_qb, g_kb):
    offs = np.cumsum((HD, HD, HD, HD, D_IDX, N_HEADS, HD, HD, HD, N_HEADS, D_MODEL, D_MODEL))[:-1].tolist()
    qa, ka, va, qi, ki, wi, qb, kb, vb, fb, ga, gb = jnp.split(w_in, offs, axis=1)
    pad = jnp.zeros((w_in.shape[0], LANES - D_IDX - 2 * N_HEADS), w_in.dtype)
    w = jnp.concatenate([qa, ka, va, qi, qb, kb, vb, ga, gb, ki, wi, fb, pad], axis=1).astype(BF16)
    tile8 = lambda g: jnp.tile(g, N_HEADS)
    gains = jnp.stack([tile8(g_qa), tile8(g_ka), tile8(g_qb), tile8(g_kb)]
                      + [jnp.zeros((HD,), F32)] * 4)
    gsm = jnp.concatenate([g_kidx, jnp.zeros((LANES - D_IDX,), F32)]).reshape(1, LANES)
    bfv = jnp.zeros((LANES,), F32).at[L_LF:L_D].set(b_f).reshape(1, LANES)
    blk = np.kron(np.eye(N_HEADS), np.full((HEAD_DIM, HEAD_DIM), 1.0 / HEAD_DIM))
    gmat = jnp.asarray(blk, BF16)
    gs = np.zeros((LANES, LANES))
    gs[:D_IDX, :D_IDX] = 1.0 / D_IDX
    return w, gains, gsm, bfv, gmat, jnp.asarray(gs, BF16)


def _proj(x, g_norm, shift, scale, pw, tables, tm, qi_dtype=BF16):
    w, gains, gsm, bfv, gmat, gsmat = pw
    b, t, d = x.shape
    per_row = shift.shape[1] != 1
    mod_spec = (pl.BlockSpec((1, tm, d), lambda bi, i: (bi, i, 0)) if per_row
                else pl.BlockSpec((1, 1, d), lambda bi, i: (bi, 0, 0)))
    const = lambda shape: pl.BlockSpec(shape, lambda bi, i: tuple(0 for _ in shape))
    tab_spec = pl.BlockSpec((tm, LANES), lambda bi, i: (i, 0))
    row = lambda width: pl.BlockSpec((1, tm, width), lambda bi, i: (bi, i, 0))
    f = lambda width, dt: jax.ShapeDtypeStruct((b, t, width), dt)
    return pl.pallas_call(
        _proj_kernel,
        grid=(b, t // tm),
        in_specs=[row(d), const((1, d)), mod_spec, mod_spec, const(w.shape), const(gains.shape),
                  const((1, LANES)), const((1, LANES)), tab_spec, tab_spec, tab_spec,
                  const(gmat.shape), const(gsmat.shape)],
        out_specs=[row(HD)] * 4 + [row(LANES)] + [row(HD)] * 7 + [row(LANES)] + [row(d)] * 2,
        out_shape=[f(HD, F32)] * 4 + [f(LANES, F32)] + [f(HD, BF16), f(HD, qi_dtype)]
                  + [f(HD, BF16)] * 5 + [f(LANES, BF16)] + [f(d, BF16)] * 2,
        scratch_shapes=[pltpu.VMEM((1, LANES), F32)],
        compiler_params=_cparams(("arbitrary", "arbitrary")),
        name="proj",
    )(x, g_norm.reshape(1, d), shift, scale, w, gains, gsm, bfv, *tables, gmat, gsmat)


DSA_TQ = 256
DSA_KC = 512
DSA_SC = 512
HEAD_GROUP = 4


def _pair_masked(q, h):
    pair = q[:, (h // 2) * LANES:(h // 2 + 1) * LANES].astype(F32)
    lane = lax.broadcasted_iota(I32, pair.shape, 1)
    keep = (lane < HEAD_DIM) if h % 2 == 0 else (lane >= HEAD_DIM)
    return jnp.where(keep, pair, 0.0).astype(BF16)


def _key_to_f32(key):
    return pltpu.bitcast(key ^ ((key >> 31) & jnp.int32(0x7FFFFFFF)), F32)


def _topk_threshold(count_ge, rows, k):
    zero = jnp.zeros((rows, 1), I32)
    (n0,) = count_ge([_key_to_f32(zero)])
    t0 = jnp.where(n0 >= k, zero, jnp.full((rows, 1), INT_MIN, I32))

    def body(j, t):
        cand = t | (jnp.int32(1) << (30 - j))
        (n,) = count_ge([_key_to_f32(cand)])
        return jnp.where(n >= k, cand, t)

    t = lax.fori_loop(0, 31, body, t0)
    has_k = t > KEY_NEG_INF
    return jnp.where(has_k, _key_to_f32(t), NEG_INF), has_k


def _topk_mask_inplace(sc_scr, nch, cw, n_sel):
    rows = sc_scr.shape[0]

    def count(tests):
        def body(c, cnts):
            off = pl.multiple_of(c * cw, cw)
            s = sc_scr[:, pl.ds(off, cw)]
            return tuple(cnt + _fold_lanes(jnp.where(cmp(s, cand), 1, 0))
                         for cnt, (cmp, cand) in zip(cnts, tests))
        cnts = lax.fori_loop(0, nch, body, (jnp.zeros((rows, LANES), I32),) * len(tests))
        return [jnp.sum(cnt, axis=1, keepdims=True) for cnt in cnts]

    ge = lambda s, cand: s >= cand
    gt = lambda s, cand: s > cand
    thr, has_k = _topk_threshold(lambda cands: count([(ge, c) for c in cands]), rows, n_sel)
    n_gt, n_ge = count([(gt, thr), (ge, thr)])
    need = n_sel - n_gt
    tie = (n_ge > n_sel) & has_k

    @pl.when(jnp.max(tie.astype(I32)) > 0)
    def _():
        r_i = lax.broadcasted_iota(I32, (LANES, LANES), 0)
        c_i = lax.broadcasted_iota(I32, (LANES, LANES), 1)
        upper = jnp.where(r_i < c_i, 1.0, 0.0).astype(BF16)

        def body(c, seen):
            off = pl.multiple_of(c * LANES, LANES)
            s = sc_scr[:, pl.ds(off, LANES)]
            eq = s == thr
            eqf = jnp.where(eq, 1.0, 0.0)
            rank = seen + _dot(eqf.astype(BF16), upper)
            demote = eq & tie & (rank >= need.astype(F32))
            sc_scr[:, pl.ds(off, LANES)] = jnp.where(demote, NEG_INF, s)
            return seen + jnp.sum(eqf, axis=1, keepdims=True)

        lax.fori_loop(0, nch * (cw // LANES), body, jnp.zeros((rows, 1), F32))

    def mask_body(c, carry):
        off = pl.multiple_of(c * cw, cw)
        s = sc_scr[:, pl.ds(off, cw)]
        sc_scr[:, pl.ds(off, cw)] = jnp.where((s >= thr) & (s > NEG_INF), 0.0, NEG_INF)
        return carry

    lax.fori_loop(0, nch, mask_body, 0)


def _fold_lanes(m):
    out = m[:, 0:LANES]
    for j in range(1, m.shape[1] // LANES):
        out = out + m[:, j * LANES:(j + 1) * LANES]
    return out


def _two_pass_attention(nkc, kc, heads, qm, k_chunk, v_chunk, bias_chunk, s_scr):
    tq = qm[0].shape[0]
    ng = len(heads)

    def logits_body(c, mxs):
        off = pl.multiple_of(c * kc, kc)
        out = []
        for g, h in enumerate(heads):
            s = _dot_nt(qm[g], k_chunk(h, off)) + bias_chunk(h, off)
            s_scr[g, :, pl.ds(off, kc)] = s
            mx = mxs[g]
            for j in range(kc // LANES):
                mx = jnp.maximum(mx, s[:, j * LANES:(j + 1) * LANES])
            out.append(mx)
        return tuple(out)

    neg = jnp.full((tq, LANES), NEG_INF, F32)
    mxs = lax.fori_loop(0, nkc, logits_body, (neg,) * ng)
    ms = []
    for mx in mxs:
        m = jnp.max(mx, axis=1, keepdims=True)
        ms.append(jnp.where(m == NEG_INF, 0.0, m))

    def pv_body(c, carry):
        off = pl.multiple_of(c * kc, kc)
        out = []
        for g, h in enumerate(heads):
            l, acc = carry[g]
            p = jnp.exp2(s_scr[g, :, pl.ds(off, kc)] - ms[g])
            out.append((l + _fold_lanes(p), acc + _dot(p.astype(BF16), v_chunk(h, off))))
        return tuple(out)

    zero = jnp.zeros((tq, LANES), F32)
    res = lax.fori_loop(0, nkc, pv_body, ((zero, zero),) * ng)
    return [acc / jnp.sum(l, axis=1, keepdims=True) for l, acc in res]


def _store_pairs(o_ref, outs):
    lane = lax.broadcasted_iota(I32, outs[0].shape, 1)
    for p in range(N_HEADS // 2):
        o_ref[0, :, p * LANES:(p + 1) * LANES] = jnp.where(
            lane < HEAD_DIM, outs[2 * p], outs[2 * p + 1]).astype(o_ref.dtype)


def _dsa_prompt_kernel(qi_ref, small_ref, qa_ref, kid_ref, ka_ref, va_ref, o_ref,
                       sc_scr, s_scr, *, n_sel):
    qt = pl.program_id(1)
    tq = qi_ref.shape[1]
    kc, sc = DSA_KC, DSA_SC
    nkc = ((qt + 1) * tq + kc - 1) // kc
    small = small_ref[0]
    qi = qi_ref[0]
    qa = qa_ref[0]
    qpos = qt * tq + lax.broadcasted_iota(I32, (tq, sc), 0)
    lane_k = lax.broadcasted_iota(I32, (tq, sc), 1)
    qim = [_pair_masked(qi, h) for h in range(N_HEADS)]

    def score_body(c, carry):
        off = pl.multiple_of(c * sc, sc)
        kid = kid_ref[0, pl.ds(off, sc), :]
        acc = jnp.zeros((tq, sc), F32)
        for h in range(N_HEADS):
            d = _dot_nt(qim[h], kid)
            acc = acc + small[:, L_WI + h:L_WI + h + 1] * jnp.maximum(d, 0.0)
        sc_scr[:, pl.ds(off, sc)] = jnp.where(off + lane_k <= qpos, acc, NEG_INF)
        return carry

    lax.fori_loop(0, nkc * (kc // sc), score_body, 0)

    _topk_mask_inplace(sc_scr, nkc, kc, n_sel)

    pair = lambda ref, h, off: ref[0, pl.ds(off, kc), (h // 2) * LANES:(h // 2 + 1) * LANES]
    outs = []
    for h0 in range(0, N_HEADS, HEAD_GROUP):
        heads = list(range(h0, h0 + HEAD_GROUP))
        outs += _two_pass_attention(
            nkc, kc, heads, [_pair_masked(qa, h) for h in heads],
            functools.partial(pair, ka_ref), functools.partial(pair, va_ref),
            lambda h, off: sc_scr[:, pl.ds(off, kc)], s_scr)
    _store_pairs(o_ref, outs)


def _dsa_prompt(qi, small, qa, kid, kab, vab):
    b, t, _ = qa.shape
    tq = min(DSA_TQ, t)
    n_sel = min(TOPK_MAX, t // 4)
    rowq = lambda w: pl.BlockSpec((1, tq, w), lambda bi, i: (bi, i, 0))
    full = lambda w: pl.BlockSpec((1, t, w), lambda bi, i: (bi, 0, 0))
    return pl.pallas_call(
        functools.partial(_dsa_prompt_kernel, n_sel=n_sel),
        grid=(b, t // tq),
        in_specs=[rowq(HD), rowq(LANES), rowq(HD), full(LANES), full(HD), full(HD)],
        out_specs=rowq(HD),
        out_shape=jax.ShapeDtypeStruct((b, t, HD), BF16),
        scratch_shapes=[pltpu.VMEM((tq, t), F32), pltpu.VMEM((HEAD_GROUP, tq, t), F32)],
        compiler_params=_cparams(("arbitrary", "arbitrary")),
        name="dsa_prompt",
    )(qi, small, qa, kid, kab, vab)


FOX_TQ = 256
FOX_KC = 512


def _fox_prompt_kernel(qb_ref, small_ref, drow_ref, kb_ref, vb_ref, o_ref, s_scr):
    qt = pl.program_id(1)
    tq = qb_ref.shape[1]
    kc = FOX_KC
    nkc = ((qt + 1) * tq + kc - 1) // kc
    small = small_ref[0]
    qb = qb_ref[0]
    qpos = qt * tq + lax.broadcasted_iota(I32, (tq, kc), 0)
    lane_k = lax.broadcasted_iota(I32, (tq, kc), 1)
    pair = lambda ref, h, off: ref[0, pl.ds(off, kc), (h // 2) * LANES:(h // 2 + 1) * LANES]

    def bias(h, off):
        b = (small[:, L_D + h:L_D + h + 1] - drow_ref[0, h:h + 1, pl.ds(off, kc)]) * LOG2E
        return jnp.where(off + lane_k <= qpos, b, NEG_INF)

    outs = []
    for h0 in range(0, N_HEADS, HEAD_GROUP):
        heads = list(range(h0, h0 + HEAD_GROUP))
        outs += _two_pass_attention(
            nkc, kc, heads, [_pair_masked(qb, h) for h in heads],
            functools.partial(pair, kb_ref), functools.partial(pair, vb_ref), bias, s_scr)
    _store_pairs(o_ref, outs)


def _fox_prompt(qb, small, drow, kbb, vbb):
    b, t, _ = qb.shape
    tq = min(FOX_TQ, t)
    rowq = lambda w: pl.BlockSpec((1, tq, w), lambda bi, i: (bi, i, 0))
    full = lambda w: pl.BlockSpec((1, t, w), lambda bi, i: (bi, 0, 0))
    return pl.pallas_call(
        _fox_prompt_kernel,
        grid=(b, t // tq),
        in_specs=[rowq(HD), rowq(LANES), pl.BlockSpec((1, N_HEADS, t), lambda bi, i: (bi, 0, 0)),
                  full(HD), full(HD)],
        out_specs=rowq(HD),
        out_shape=jax.ShapeDtypeStruct((b, t, HD), BF16),
        scratch_shapes=[pltpu.VMEM((HEAD_GROUP, tq, t), F32)],
        compiler_params=_cparams(("arbitrary", "arbitrary")),
        name="fox_prompt",
    )(qb, small, drow, kbb, vbb)


def _mix_kernel(x_ref, oa_ref, ob_ref, ga_ref, gb_ref, gt1_ref, sh2_ref, sc2_ref, gn2_ref,
                woa_ref, wob_ref, wout_ref, wr_ref, br_ref, x1_o, h2_o, comb_o):
    ya = _dot(oa_ref[0], woa_ref[...])
    yb = _dot(ob_ref[0], wob_ref[...])
    mix = ga_ref[0].astype(F32) * ya + gb_ref[0].astype(F32) * yb
    x1 = x_ref[0] + gt1_ref[0] * _dot(mix.astype(BF16), wout_ref[...])
    x1_o[0] = x1
    ms = jnp.mean(x1 * x1, axis=-1, keepdims=True)
    h2 = (x1 * lax.rsqrt(ms + EPS) * gn2_ref[...]) * (1.0 + sc2_ref[0]) + sh2_ref[0]
    h2b = h2.astype(BF16)
    h2_o[0] = h2b
    wr = wr_ref[...]
    w_hi = wr.astype(BF16)
    w_lo = (wr - w_hi.astype(F32)).astype(BF16)
    h_lo = (h2 - h2b.astype(F32)).astype(BF16)
    logits = _dot(h2b, w_hi) + _dot(h2b, w_lo) + _dot(h_lo, w_hi) + br_ref[...]
    lane = lax.broadcasted_iota(I32, logits.shape, 1)
    work = logits
    vals, idxs = [], []
    for _ in range(TOP_K_EXPERTS):
        mx = jnp.max(work, axis=1, keepdims=True)
        idx = jnp.min(jnp.where(work == mx, lane, LANES), axis=1, keepdims=True)
        vals.append(mx)
        idxs.append(idx)
        work = jnp.where(lane == idx, NEG_INF, work)
    es = [jnp.exp(v - vals[0]) for v in vals]
    denom = es[0] + es[1] + es[2] + es[3]
    comb = jnp.zeros(logits.shape, F32)
    for e, idx in zip(es, idxs):
        comb = comb + jnp.where(lane == idx, e / denom, 0.0)
    comb_o[0] = comb


def _mix(x, oa, ob, ga, gb, gt1, sh2, sc2, g_norm2, mw, tm):
    woa, wob, wout, wr, br = mw
    b, t, d = x.shape
    per_row = gt1.shape[1] != 1
    mod_spec = (pl.BlockSpec((1, tm, d), lambda bi, i: (bi, i, 0)) if per_row
                else pl.BlockSpec((1, 1, d), lambda bi, i: (bi, 0, 0)))
    const = lambda shape: pl.BlockSpec(shape, lambda bi, i: tuple(0 for _ in shape))
    row = lambda width: pl.BlockSpec((1, tm, width), lambda bi, i: (bi, i, 0))
    return pl.pallas_call(
        _mix_kernel,
        grid=(b, t // tm),
        in_specs=[row(d), row(HD), row(HD), row(d), row(d), mod_spec, mod_spec, mod_spec,
                  const((1, d)), const(woa.shape), const(wob.shape), const(wout.shape),
                  const(wr.shape), const(br.shape)],
        out_specs=[row(d), row(d), row(LANES)],
        out_shape=[jax.ShapeDtypeStruct((b, t, d), F32), jax.ShapeDtypeStruct((b, t, d), BF16),
                   jax.ShapeDtypeStruct((b, t, LANES), F32)],
        compiler_params=_cparams(("arbitrary", "arbitrary")),
        name="mix_router",
    )(x, oa, ob, ga, gb, gt1, sh2, sc2, g_norm2.reshape(1, d), woa, wob, wout, wr, br)


MOE_TM = 1024
MOE_CH = 160
MOE_RB = 256
MOE_WSPLIT = 2


def _moe_kernel(h_ref, comb_ref, x1_ref, gt2_ref, *refs):
    wup_refs, refs = refs[:2 * MOE_WSPLIT], refs[2 * MOE_WSPLIT:]
    bup_ref, refs = refs[0], refs[1:]
    wdn_refs, refs = refs[:MOE_WSPLIT], refs[MOE_WSPLIT:]
    bdn_ref, y_o, rank_t_ref, comb_t_ref = refs
    wb = D_FF // MOE_WSPLIT
    e = pl.program_id(2)
    tm = h_ref.shape[1]
    ch = MOE_CH
    rb = min(MOE_RB, tm)

    @pl.when(e == 0)
    def _():
        y_o[0] = jnp.zeros(y_o.shape[1:], F32)
        comb = comb_ref[0]
        sel = jnp.where(comb > 0.0, 1.0, 0.0).astype(BF16)
        eye = jnp.where(lax.broadcasted_iota(I32, (LANES, LANES), 0)
                        == lax.broadcasted_iota(I32, (LANES, LANES), 1), 1.0, 0.0).astype(BF16)
        sel_t = _dot_nt(eye, sel).astype(BF16)
        hi, mid, lo = _split3(comb)
        comb_t_ref[...] = _dot_nt(eye, hi) + _dot_nt(eye, mid) + _dot_nt(eye, lo)
        for blk in range(tm // rb):
            r_j = lax.broadcasted_iota(I32, (tm, rb), 0)
            c_j = blk * rb + lax.broadcasted_iota(I32, (tm, rb), 1)
            earlier_t = jnp.where(r_j < c_j, 1.0, 0.0).astype(BF16)
            rank_t_ref[:, blk * rb:(blk + 1) * rb] = _dot(sel_t, earlier_t)

    rank_row = rank_t_ref[pl.ds(e, 1), :]
    c_row = comb_t_ref[pl.ds(e, 1), :]
    n_tok = jnp.sum(jnp.where(c_row > 0.0, 1.0, 0.0)).astype(I32)

    def chunk(c, carry):
        base = (c * ch).astype(F32)
        slot_r = base + lax.broadcasted_iota(I32, (ch, tm), 0).astype(F32)
        pf = jnp.where((rank_row == slot_r) & (c_row > 0.0), 1.0, 0.0)
        c_sorted = jnp.sum(pf * c_row, axis=1, keepdims=True)
        xg = _dot(pf.astype(BF16), h_ref[0]).astype(BF16)
        ye = jnp.zeros((ch, y_o.shape[2]), F32)
        for j in range(MOE_WSPLIT):
            bg = bup_ref[0, :, j * wb:(j + 1) * wb]
            bl = bup_ref[0, :, D_FF + j * wb:D_FF + (j + 1) * wb]
            g = jnp.minimum(_dot(xg, wup_refs[j][0]) + bg, SWIGLU_LIMIT)
            lin = jnp.clip(_dot(xg, wup_refs[MOE_WSPLIT + j][0]) + bl, -SWIGLU_LIMIT, SWIGLU_LIMIT)
            act = (lin + 1.0) * g * _sigmoid(SWIGLU_ALPHA * g)
            ye = ye + _dot(act.astype(BF16), wdn_refs[j][0])
        yw = (c_sorted * (ye + bdn_ref[0])).astype(BF16)
        y_o[0] += _dot(pf.T.astype(BF16), yw)
        return carry

    lax.fori_loop(0, (n_tok + ch - 1) // ch, chunk, 0)

    @pl.when(e == pl.num_programs(2) - 1)
    def _():
        y_o[0] = x1_ref[0] + gt2_ref[0] * y_o[0]


def _moe(h2, comb, x1, gt2, ew, tm):
    wup, bup, wdn, bdn = ew
    b, t, d = x1.shape
    ne = wup.shape[0]
    wb = D_FF // MOE_WSPLIT
    per_row = gt2.shape[1] != 1
    mod_spec = (pl.BlockSpec((1, tm, d), lambda bi, i, e: (bi, i, 0)) if per_row
                else pl.BlockSpec((1, 1, d), lambda bi, i, e: (bi, 0, 0)))
    row = lambda width: pl.BlockSpec((1, tm, width), lambda bi, i, e: (bi, i, 0))
    return pl.pallas_call(
        _moe_kernel,
        grid=(b, t // tm, ne),
        in_specs=[row(d), row(LANES), row(d), mod_spec]
                 + [pl.BlockSpec((1, d, wb), functools.partial(lambda j, bi, i, e: (e, 0, j), j))
                    for j in range(2 * MOE_WSPLIT)]
                 + [pl.BlockSpec((1, 1, 2 * D_FF), lambda bi, i, e: (e, 0, 0))]
                 + [pl.BlockSpec((1, wb, d), functools.partial(lambda j, bi, i, e: (e, j, 0), j))
                    for j in range(MOE_WSPLIT)]
                 + [pl.BlockSpec((1, 1, d), lambda bi, i, e: (e, 0, 0))],
        out_specs=row(d),
        out_shape=jax.ShapeDtypeStruct((b, t, d), F32),
        scratch_shapes=[pltpu.VMEM((LANES, tm), F32), pltpu.VMEM((LANES, tm), F32)],
        compiler_params=_cparams(("arbitrary", "arbitrary", "arbitrary")),
        name="moe",
    )(h2, comb, x1, gt2, *([wup] * (2 * MOE_WSPLIT)), bup, *([wdn] * MOE_WSPLIT), bdn)


def _mod_split(mod, lo, hi):
    return jnp.split(mod[lo:hi], N_ADA, axis=-1)


def _prompt_group(x, mods, pw, mw, ew, g_norm1, g_norm2):
    b, t, d = x.shape
    sh1, sc1, gt1, sh2, sc2, gt2 = [m[:, None, :] for m in mods]
    tables = _rope_tables(jnp.arange(t, dtype=I32))
    (ka, va, kb, vb, small, qa, qi, qb, kab, vab, kbb, vbb, kid, ga, gb) = _proj(
        x, g_norm1, sh1, sc1, pw, tables, tm=min(256, t))
    out_a = _dsa_prompt(qi, small, qa, kid, kab, vab)
    drow = jnp.swapaxes(small[:, :, L_D:L_D + N_HEADS], 1, 2)
    out_b = _fox_prompt(qb, small, drow, kbb, vbb)
    x1, h2, comb = _mix(x, out_a, out_b, ga, gb, gt1, sh2, sc2, g_norm2, mw, tm=min(256, t))
    y = _moe(h2, comb, x1, gt2, ew, tm=min(MOE_TM, t))
    heads = lambda z: z.reshape(1, b, t, N_HEADS, HEAD_DIM)
    return (y, heads(ka), heads(va), small[None, :, :, :D_IDX], heads(kb), heads(vb),
            small[None, :, :, L_LF:L_D])


def _mix_weights(w_oa, w_ob, w_out, w_router, b_router):
    ne = w_router.shape[1]
    wr = jnp.concatenate([w_router, jnp.zeros((w_router.shape[0], LANES - ne), w_router.dtype)], axis=1)
    br = jnp.concatenate([b_router, jnp.full((LANES - ne,), NEG_INF, b_router.dtype)]).reshape(1, LANES)
    return w_oa.astype(BF16), w_ob.astype(BF16), w_out.astype(BF16), wr, br


def _moe_weights(w_up, b_up, w_down, b_down):
    ne = w_up.shape[0]
    return (w_up.astype(BF16), b_up.reshape(ne, 1, -1), w_down.astype(BF16), b_down.reshape(ne, 1, -1))


SCORE_CHUNK = 1024


def _page_gather(pt_ref, b, cache_ref, dst_slab, sem):
    npages = pt_ref.shape[1]

    def copy(p, page):
        return pltpu.make_async_copy(cache_ref.at[page], dst_slab(p), sem)

    def issue(p, c):
        copy(p, pt_ref[b, p]).start()
        return c

    def wait(p, c):
        copy(p, 0).wait()
        return c

    lax.fori_loop(0, npages, issue, 0)
    lax.fori_loop(0, npages, wait, 0)


def _sidx_kernel(pt_ref, q8_ref, w8_ref, knew_ref, cache_ref, sc_ref, kbuf, sem):
    b = pl.program_id(0)
    _page_gather(pt_ref, b, cache_ref,
                 lambda p: kbuf.at[:, pl.ds(pl.multiple_of(p * PAGE_SIZE, PAGE_SIZE), PAGE_SIZE)], sem)
    n_past = kbuf.shape[1]
    q8 = q8_ref[0]
    w8 = w8_ref[0]
    q_hi = q8.astype(BF16)
    q_lo = (q8 - q_hi.astype(F32)).astype(BF16)
    for c in range(n_past // SCORE_CHUNK):
        kf = kbuf[:, c * SCORE_CHUNK:(c + 1) * SCORE_CHUNK]
        k_hi = kf.astype(BF16)
        k_lo = (kf - k_hi.astype(F32)).astype(BF16)
        d = _dot(q_hi, k_hi) + _dot(q_hi, k_lo) + _dot(q_lo, k_hi)
        sc_ref[0, :, c * SCORE_CHUNK:(c + 1) * SCORE_CHUNK] = jnp.sum(
            w8 * jnp.maximum(d, 0.0), axis=0, keepdims=True)
    dn = jnp.sum(q8 * knew_ref[0], axis=1, keepdims=True)
    sn = jnp.sum(w8 * jnp.maximum(dn, 0.0), axis=0, keepdims=True)
    lane = lax.broadcasted_iota(I32, (1, LANES), 1)
    sc_ref[0, :, n_past:n_past + LANES] = jnp.where(lane == 0, sn, NEG_INF)


def _sample_scores(page_table, q8, w8, knew, cache_kidx):
    nb, npages = page_table.shape
    n_past = npages * PAGE_SIZE
    grid_spec = pltpu.PrefetchScalarGridSpec(
        num_scalar_prefetch=1,
        grid=(nb,),
        in_specs=[pl.BlockSpec((1, N_HEADS, D_IDX), lambda b, pt: (b, 0, 0)),
                  pl.BlockSpec((1, N_HEADS, 1), lambda b, pt: (b, 0, 0)),
                  pl.BlockSpec((1, 1, D_IDX), lambda b, pt: (b, 0, 0)),
                  pl.BlockSpec(memory_space=pl.ANY)],
        out_specs=pl.BlockSpec((1, 1, n_past + LANES), lambda b, pt: (b, 0, 0)),
        scratch_shapes=[pltpu.VMEM((D_IDX, n_past), F32), pltpu.SemaphoreType.DMA(())],
    )
    return pl.pallas_call(
        _sidx_kernel,
        grid_spec=grid_spec,
        out_shape=jax.ShapeDtypeStruct((nb, 1, n_past + LANES), F32),
        compiler_params=_cparams(("arbitrary",)),
        name="sample_scores",
    )(page_table, q8, w8, knew, cache_kidx)


def _ssel_kernel(sc_ref, o_ref, *, n_sel):
    o_ref[...] = sc_ref[...]
    _topk_mask_inplace(o_ref, o_ref.shape[1] // LANES, LANES, n_sel)


def _sample_select(scores, n_keys):
    rows, n = scores.shape
    n_sel = min(TOPK_MAX, n_keys // 4)
    return pl.pallas_call(
        functools.partial(_ssel_kernel, n_sel=n_sel),
        out_shape=jax.ShapeDtypeStruct((rows, n), F32),
        compiler_params=pltpu.CompilerParams(vmem_limit_bytes=VMEM_LIMIT),
        name="sample_select",
    )(scores)


def _split3(x):
    hi = x.astype(BF16)
    r1 = x - hi.astype(F32)
    mid = r1.astype(BF16)
    lo = (r1 - mid.astype(F32)).astype(BF16)
    return hi, mid, lo


def _dot3_l(x, m):
    hi, mid, lo = _split3(x)
    return _dot(hi, m) + _dot(mid, m) + _dot(lo, m)


def _dot3_r(m, x):
    hi, mid, lo = _split3(x)
    return _dot(m, hi) + _dot(m, mid) + _dot(m, lo)


def _fbias_kernel(pt_ref, lfnew_ref, cache_ref, o_ref, lbuf, sem):
    b = pl.program_id(0)
    _page_gather(pt_ref, b, cache_ref, lambda p: lbuf.at[p], sem)
    npages = lbuf.shape[0]
    r_i = lax.broadcasted_iota(I32, (npages, npages), 0)
    c_i = lax.broadcasted_iota(I32, (npages, npages), 1)
    later_pages = jnp.where(c_i > r_i, 1.0, 0.0).astype(BF16)
    t_r = lax.broadcasted_iota(I32, (PAGE_SIZE, PAGE_SIZE), 0)
    t_c = lax.broadcasted_iota(I32, (PAGE_SIZE, PAGE_SIZE), 1)
    later_toks = jnp.where(t_r > t_c, 1.0, 0.0).astype(BF16)
    lfnew = lfnew_ref[0]
    for h in range(N_HEADS):
        x = lbuf[:, h, :]
        within = _dot3_l(x, later_toks)
        tot = jnp.sum(x, axis=1, keepdims=True)
        later = _dot3_r(later_pages, jnp.broadcast_to(tot, (npages, PAGE_SIZE)))
        o_ref[0, h] = (within + later + lfnew[:, h:h + 1]) * LOG2E


def _sample_fox_bias(page_table, lfnew, cache_logf_t):
    nb, npages = page_table.shape
    grid_spec = pltpu.PrefetchScalarGridSpec(
        num_scalar_prefetch=1,
        grid=(nb,),
        in_specs=[pl.BlockSpec((1, 1, N_HEADS), lambda b, pt: (b, 0, 0)),
                  pl.BlockSpec(memory_space=pl.ANY)],
        out_specs=pl.BlockSpec((1, N_HEADS, npages, PAGE_SIZE), lambda b, pt: (b, 0, 0, 0)),
        scratch_shapes=[pltpu.VMEM((npages, N_HEADS, PAGE_SIZE), F32), pltpu.SemaphoreType.DMA(())],
    )
    return pl.pallas_call(
        _fbias_kernel,
        grid_spec=grid_spec,
        out_shape=jax.ShapeDtypeStruct((nb, N_HEADS, npages, PAGE_SIZE), F32),
        compiler_params=_cparams(("arbitrary",)),
        name="sample_fox_bias",
    )(page_table, lfnew, cache_logf_t)


DEC_PAGES = 32
QROWS = 16


def _decode_kernel(pt_ref, q_ref, knew_ref, vnew_ref, bnew_ref, *refs, npg):
    k_refs, v_refs, b_refs = refs[:npg], refs[npg:2 * npg], refs[2 * npg:3 * npg]
    o_ref, m_scr, l_scr, acc_scr = refs[3 * npg:]
    s = pl.program_id(1)
    hrow = lax.broadcasted_iota(I32, (QROWS, HD), 0)
    lane = lax.broadcasted_iota(I32, (QROWS, HD), 1)
    own = (lane // HEAD_DIM) == hrow
    qf = jnp.where(own, jnp.broadcast_to(q_ref[0].astype(F32), (QROWS, HD)), 0.0)
    qm = qf.astype(BF16)

    @pl.when(s == 0)
    def _():
        sn = jnp.sum(qf[:N_HEADS] * knew_ref[0], axis=1, keepdims=True) + bnew_ref[0]
        live = sn > NEG_INF
        m_scr[...] = jnp.broadcast_to(sn, m_scr.shape)
        l_scr[...] = jnp.broadcast_to(jnp.where(live, 1.0, 0.0), l_scr.shape)
        acc_scr[...] = jnp.where(live, 1.0, 0.0) * jnp.broadcast_to(vnew_ref[0], acc_scr.shape)

    sc = jnp.concatenate(
        [_dot(qm, k_refs[j][0].astype(BF16))[:N_HEADS] + b_refs[j][0, 0] for j in range(npg)], axis=1)
    m_old = m_scr[...]
    m_new = jnp.maximum(m_old, jnp.max(sc, axis=1, keepdims=True))
    m_safe = jnp.where(m_new == NEG_INF, 0.0, m_new)
    p = jnp.exp2(sc - m_safe[:, 0:1])
    alpha = jnp.exp2(m_old - m_safe)
    l_scr[...] = alpha * l_scr[...] + jnp.sum(p, axis=1, keepdims=True)
    p16 = jnp.concatenate([p, jnp.zeros_like(p)], axis=0).astype(BF16)
    pv = _dot_nt(p16[:, 0:PAGE_SIZE], v_refs[0][0].astype(BF16))
    for j in range(1, npg):
        pv = pv + _dot_nt(p16[:, j * PAGE_SIZE:(j + 1) * PAGE_SIZE], v_refs[j][0].astype(BF16))
    acc_scr[...] = alpha[:, 0:1] * acc_scr[...] + pv[:N_HEADS]
    m_scr[...] = m_new

    @pl.when(s == pl.num_programs(1) - 1)
    def _():
        o = jnp.where(own[:N_HEADS], acc_scr[...] / l_scr[:, 0:1], 0.0)
        o_ref[0] = jnp.sum(o, axis=0, keepdims=True)


def _paged_decode(page_table, q, knew, vnew, bnew, cache_k, cache_v, bias):
    nb, npages = page_table.shape
    hb = bias.shape[2]
    npg = min(DEC_PAGES, npages)
    kv_spec = lambda j: pl.BlockSpec((1, HD, PAGE_SIZE), lambda b, s, pt: (pt[b, s * npg + j], 0, 0))
    b_spec = lambda j: pl.BlockSpec((1, 1, hb, PAGE_SIZE), lambda b, s, pt: (b, s * npg + j, 0, 0))
    row = lambda w: pl.BlockSpec((1, 1, w), lambda b, s, pt: (b, 0, 0))
    grid_spec = pltpu.PrefetchScalarGridSpec(
        num_scalar_prefetch=1,
        grid=(nb, npages // npg),
        in_specs=[row(HD), row(HD), row(HD), pl.BlockSpec((1, N_HEADS, 1), lambda b, s, pt: (b, 0, 0))]
                 + [kv_spec(j) for j in range(npg)] * 2 + [b_spec(j) for j in range(npg)],
        out_specs=row(HD),
        scratch_shapes=[pltpu.VMEM((N_HEADS, LANES), F32), pltpu.VMEM((N_HEADS, LANES), F32),
                        pltpu.VMEM((N_HEADS, HD), F32)],
    )
    return pl.pallas_call(
        functools.partial(_decode_kernel, npg=npg),
        grid_spec=grid_spec,
        out_shape=jax.ShapeDtypeStruct((nb, 1, HD), F32),
        compiler_params=_cparams(("arbitrary", "arbitrary")),
        name="paged_decode",
    )(page_table, q, knew, vnew, bnew, *([cache_k] * npg), *([cache_v] * npg), *([bias] * npg))


def _sample_group(x, mods, pw, mw, ew, g_norm1, g_norm2, caches, page_table, pos):
    nb, t, d = x.shape
    cache_k_a, cache_v_a, cache_kidx, cache_k_b, cache_v_b, cache_logf = caches
    n_pool = cache_k_a.shape[0]
    npages = page_table.shape[1]
    n_past = npages * PAGE_SIZE
    sh1, sc1, gt1, sh2, sc2, gt2 = [m[None] for m in mods]
    xs = x.reshape(1, nb, d)
    tables = _rope_tables(jnp.full((nb,), pos, I32))
    (ka, va, kb, vb, small, qa, qi, qb, _, _, _, _, _, ga, gb) = _proj(
        xs, g_norm1, sh1, sc1, pw, tables, tm=nb, qi_dtype=F32)
    col = lambda z: z.reshape(nb, 1, -1)
    kv_t = lambda c: jnp.transpose(c, (0, 2, 3, 1)).reshape(n_pool, HD, PAGE_SIZE)
    scores = _sample_scores(page_table, qi.reshape(nb, N_HEADS, D_IDX),
                            small[0, :, L_WI:L_LF].reshape(nb, N_HEADS, 1),
                            small[0, :, :D_IDX].reshape(nb, 1, D_IDX), jnp.swapaxes(cache_kidx, 1, 2))
    mask = _sample_select(scores.reshape(nb, -1), n_past + t)
    bias_a = mask[:, :n_past].reshape(nb, npages, 1, PAGE_SIZE)
    bnew_a = jnp.broadcast_to(mask[:, n_past:n_past + 1, None], (nb, N_HEADS, 1))
    out_a = _paged_decode(page_table, col(qa), col(ka), col(va), bnew_a,
                          kv_t(cache_k_a), kv_t(cache_v_a), bias_a)
    logf = small[0, :, L_LF:L_D]
    bias_b = _sample_fox_bias(page_table, logf.reshape(nb, 1, N_HEADS), jnp.swapaxes(cache_logf, 1, 2))
    out_b = _paged_decode(page_table, col(qb), col(kb), col(vb), jnp.zeros((nb, N_HEADS, 1), F32),
                          kv_t(cache_k_b), kv_t(cache_v_b), jnp.swapaxes(bias_b, 1, 2))
    oa = out_a.reshape(1, nb, HD).astype(BF16)
    ob = out_b.reshape(1, nb, HD).astype(BF16)
    x1, h2, comb = _mix(xs, oa, ob, ga, gb, gt1, sh2, sc2, g_norm2, mw, tm=nb)
    rows = -(-nb // MOE_RB) * MOE_RB
    padr = lambda z: jnp.pad(z, ((0, 0), (0, rows - nb), (0, 0)))
    y = _moe(padr(h2), padr(comb), padr(x1), padr(gt2), ew, tm=rows)[:, :nb]
    heads = lambda z: z.reshape(1, nb, t, N_HEADS, HEAD_DIM)
    return (y.reshape(nb, t, d), heads(ka), heads(va), small[0, :, :D_IDX].reshape(1, nb, t, D_IDX),
            heads(kb), heads(vb), logf.reshape(1, nb, t, N_HEADS))


def kernel(x_prompt, x_sample, c_prompt, c_sample, cache_k_a, cache_v_a, cache_kidx_a, cache_k_b,
           cache_v_b, cache_logf_b, page_table, w_ada, b_ada, g_norm1, w_in, b_f, g_qa, g_ka, g_kidx,
           g_qb, g_kb, w_oa, w_ob, w_out, g_norm2, w_router, b_router, w_up, b_up, w_down, b_down):
    depth = w_in.shape[0]
    assert depth == 1, "single-layer trunk"
    l = 0
    nbp, nbs = x_prompt.shape[0], x_sample.shape[0]
    n_past = page_table.shape[1] * PAGE_SIZE
    rows = nbp + nbs
    pad = (-rows) % 8
    c_all = jnp.concatenate([c_prompt, c_sample, jnp.zeros((pad, c_prompt.shape[1]), F32)], axis=0)
    mod = _adaln(c_all, w_ada[l], b_ada[l])
    pw = _proj_weights(w_in[l], b_f[l], g_qa[l], g_ka[l], g_kidx[l], g_qb[l], g_kb[l])
    mw = _mix_weights(w_oa[l], w_ob[l], w_out[l], w_router[l], b_router[l])
    ew = _moe_weights(w_up[l], b_up[l], w_down[l], b_down[l])
    outs_p = _prompt_group(x_prompt, _mod_split(mod, 0, nbp), pw, mw, ew, g_norm1[l], g_norm2[l])
    caches = (cache_k_a[l], cache_v_a[l], cache_kidx_a[l], cache_k_b[l], cache_v_b[l], cache_logf_b[l])
    outs_s = _sample_group(x_sample, _mod_split(mod, nbp, rows), pw, mw, ew, g_norm1[l], g_norm2[l],
                           caches, page_table, n_past)
    return (outs_p[0], outs_s[0]) + tuple(outs_p[1:]) + tuple(outs_s[1:])
```
